```python
import math
import jax, jax.numpy as jnp
from jax import lax
import numpy as np

D_MODEL = 1024
BATCH = 4
SEQ = 8192
DEPTH = 2

NSA_HEADS = 8
NSA_KV_HEADS = 2
NSA_HEAD_DIM = 64
NSA_CMP_BLOCK = 32
NSA_CMP_STRIDE = 16
NSA_SEL_BLOCK = 64
NSA_TOP_N = 16
NSA_WINDOW = 512
NSA_CMP_HIDDEN = 256
NSA_Q_BLOCK = 128
NSA_BIG = 1e9
ROPE_THETA = 500000.0
ROPE_DIM = NSA_HEAD_DIM // 4
NSA_WIDTH = NSA_HEADS * NSA_HEAD_DIM
NSA_KV_WIDTH = NSA_KV_HEADS * NSA_HEAD_DIM
HG_HEADS = 4
HG_HEAD_DIM = 128
HG_WIDTH = HG_HEADS * HG_HEAD_DIM
HG_CHUNK = 64
M2_HEADS = 16
M2_HEAD_DIM = 64
M2_INNER = M2_HEADS * M2_HEAD_DIM
M2_GROUPS = 2
M2_STATE = 128
M2_CONV = 4
M2_CHUNK = 128
M2_XBC = M2_INNER + 2 * M2_GROUPS * M2_STATE
FFN_DIM = 2816
FFN_CONV = 3
EPS = 1e-6
IN_SPLITS = (NSA_WIDTH, 6 * NSA_KV_WIDTH, 3 * NSA_HEADS,
             HG_WIDTH, HG_WIDTH, HG_WIDTH, HG_WIDTH,
             M2_INNER, M2_XBC, M2_HEADS,
             3 * D_MODEL)
IN_DIM = NSA_WIDTH + 6 * NSA_KV_WIDTH + 3 * NSA_HEADS + 4 * HG_WIDTH + M2_INNER + M2_XBC + M2_HEADS + 3 * D_MODEL

kernel_name = 'hybrid_nsa_hgrn2_ssd_convffn'


def rms_norm(x, g):
    xf = x.astype(jnp.float32)
    y = xf * lax.rsqrt(jnp.mean(xf * xf, axis=-1, keepdims=True) + EPS)
    return (y * g).astype(x.dtype)


def split_cols(a, sizes):
    offsets = np.cumsum(np.array(sizes))[:-1].tolist()
    return jnp.split(a, offsets, axis=-1)


def causal_dwconv(x, w, b):
    width, ch = w.shape
    y = lax.conv_general_dilated(x, w.astype(x.dtype)[:, None, :], window_strides=(1,),
                                 padding=[(width - 1, 0)],
                                 dimension_numbers=('NWC', 'WIO', 'NWC'),
                                 feature_group_count=ch)
    return y + b.astype(y.dtype)


def partial_rope(x, positions):
    half = ROPE_DIM // 2
    inv_freq = ROPE_THETA ** (-jnp.arange(0, ROPE_DIM, 2, dtype=jnp.float32) / ROPE_DIM)
    ang = positions.astype(jnp.float32)[..., None] * inv_freq
    cos, sin = jnp.cos(ang)[:, :, None, :], jnp.sin(ang)[:, :, None, :]
    x1, x2, rest = x[..., :half], x[..., half:ROPE_DIM], x[..., ROPE_DIM:]
    rot = jnp.concatenate([x1 * cos - x2 * sin, x2 * cos + x1 * sin], axis=-1)
    return jnp.concatenate([rot.astype(x.dtype), rest], axis=-1)


def masked_softmax(s, mask):
    s = jnp.where(mask, s.astype(jnp.float32), -jnp.inf)
    m = jnp.max(s, axis=-1, keepdims=True)
    m = jnp.where(jnp.isfinite(m), m, 0.0)
    e = jnp.exp(s - m)
    return e / jnp.maximum(jnp.sum(e, axis=-1, keepdims=True), 1e-30)


def segsum(a):
    t = a.shape[-1]
    cs = jnp.cumsum(a, axis=-1)
    diff = cs[..., :, None] - cs[..., None, :]
    return jnp.where(jnp.tril(jnp.ones((t, t), dtype=bool)), diff, -jnp.inf)


def nsa_mixer(q_cols, kv_cols, gate_cols, positions, q_norm_g, k_norm_g, cmp_pos_k, cmp_pos_v,
              cmp_k_w1, cmp_k_w2, cmp_v_w1, cmp_v_w2):
    b, s, _ = q_cols.shape
    g, r, dh, tq = NSA_KV_HEADS, NSA_HEADS // NSA_KV_HEADS, NSA_HEAD_DIM, NSA_Q_BLOCK
    scale = dh ** -0.5
    q = rms_norm(q_cols.reshape(b, s, NSA_HEADS, dh), q_norm_g)
    q_rot = partial_rope(q, positions)
    kv = kv_cols.reshape(b, s, 3, 2, g, dh)

    n_cmp = (s - NSA_CMP_BLOCK) // NSA_CMP_STRIDE + 1
    c_start_np = np.arange(n_cmp) * NSA_CMP_STRIDE
    blk = c_start_np[:, None] + np.arange(NSA_CMP_BLOCK)[None, :]

    def compress(t, pos_emb, w1, w2):
        tb = t[:, blk] + pos_emb[None, None, :, None, :]
        tb = tb.transpose(0, 1, 3, 2, 4).reshape(b, n_cmp, g, NSA_CMP_BLOCK * dh)
        return jax.nn.gelu(tb @ w1) @ w2

    k_cmp = rms_norm(compress(kv[:, :, 0, 0], cmp_pos_k, cmp_k_w1, cmp_k_w2), k_norm_g[0])
    v_cmp = compress(kv[:, :, 0, 1], cmp_pos_v, cmp_v_w1, cmp_v_w2)
    k_sel = partial_rope(rms_norm(kv[:, :, 1, 0], k_norm_g[1]), positions).transpose(0, 2, 1, 3)
    v_sel = kv[:, :, 1, 1].transpose(0, 2, 1, 3)
    pad = ((0, 0), (NSA_WINDOW, 0), (0, 0), (0, 0))
    k_win = jnp.pad(partial_rope(rms_norm(kv[:, :, 2, 0], k_norm_g[2]), positions), pad)
    v_win = jnp.pad(kv[:, :, 2, 1], pad)
    gates = jax.nn.sigmoid(gate_cols.astype(jnp.float32)).reshape(b, s, g, r, 3)

    n_sel = s // NSA_SEL_BLOCK
    top_n = min(NSA_TOP_N, n_sel)
    s_start_np = np.arange(n_sel) * NSA_SEL_BLOCK
    overlap = np.clip(np.minimum(c_start_np[:, None] + NSA_CMP_BLOCK, s_start_np[None, :] + NSA_SEL_BLOCK)
                      - np.maximum(c_start_np[:, None], s_start_np[None, :]), 0, None)
    cmp_to_sel = jnp.asarray(overlap / NSA_CMP_BLOCK, dtype=jnp.float32)
    c_end = jnp.asarray(c_start_np + NSA_CMP_BLOCK - 1)
    s_start = jnp.asarray(s_start_np)
    blk_ids = jnp.arange(n_sel)
    within = jnp.arange(NSA_SEL_BLOCK)
    b_idx = jnp.arange(b)[:, None, None]
    g_idx = jnp.arange(g)[None, :, None]

    def query_block(i):
        s0 = i * tq
        t = s0 + jnp.arange(tq)
        qb = lax.dynamic_slice_in_dim(q, s0, tq, 1).reshape(b, tq, g, r, dh)
        qr = lax.dynamic_slice_in_dim(q_rot, s0, tq, 1).reshape(b, tq, g, r, dh)
        gb = lax.dynamic_slice_in_dim(gates, s0, tq, 1)
        sc = jnp.einsum('btgrd,bjgd->bgrtj', qb, k_cmp) * scale
        p_cmp = masked_softmax(sc, c_end[None, :] <= t[:, None])
        o_cmp = jnp.einsum('bgrtj,bjgd->btgrd', p_cmp, v_cmp)
        imp = jnp.einsum('bgrtj,jn->bgtn', p_cmp, cmp_to_sel)
        cur = t // NSA_SEL_BLOCK
        forced = (blk_ids[None, :] == 0) | (blk_ids[None, :] == cur[:, None]) | (blk_ids[None, :] == cur[:, None] - 1)
        valid = s_start[None, :] <= t[:, None]
        imp = jnp.where(forced, NSA_BIG, jnp.where(valid, imp, -NSA_BIG))
        _, sel = lax.top_k(imp, top_n)
        key_pos = (sel[..., None] * NSA_SEL_BLOCK + within).reshape(b, g, tq * top_n * NSA_SEL_BLOCK)
        ks = k_sel[b_idx, g_idx, key_pos].reshape(b, g, tq, top_n * NSA_SEL_BLOCK, dh)
        vs = v_sel[b_idx, g_idx, key_pos].reshape(b, g, tq, top_n * NSA_SEL_BLOCK, dh)
        ss = jnp.einsum('btgrd,bgtkd->bgrtk', qr, ks) * scale
        sel_mask = (key_pos.reshape(b, g, tq, -1) <= t[:, None])[:, :, None]
        o_sel = jnp.einsum('bgrtk,bgtkd->btgrd', masked_softmax(ss, sel_mask), vs)
        kw = lax.dynamic_slice_in_dim(k_win, s0, NSA_WINDOW + tq, 1)
        vw = lax.dynamic_slice_in_dim(v_win, s0, NSA_WINDOW + tq, 1)
        kpos = s0 - NSA_WINDOW + jnp.arange(NSA_WINDOW + tq)
        wmask = (kpos[None, :] >= 0) & (kpos[None, :] <= t[:, None]) & (t[:, None] - kpos[None, :] < NSA_WINDOW)
        sw = jnp.einsum('btgrd,bkgd->bgrtk', qr, kw) * scale
        o_win = jnp.einsum('bgrtk,bkgd->btgrd', masked_softmax(sw, wmask), vw)
        o = gb[..., 0:1] * o_cmp + gb[..., 1:2] * o_sel + gb[..., 2:3] * o_win
        return o.reshape(b, tq, NSA_WIDTH)

    out = lax.map(query_block, jnp.arange(s // tq))
    return out.transpose(1, 0, 2, 3).reshape(b, s, NSA_WIDTH)


def hgrn2_mixer(q_cols, f_cols, i_cols, g_cols, lower_bound, norm_g):
    b, s, _ = q_cols.shape
    h, dk, c = HG_HEADS, HG_HEAD_DIM, HG_CHUNK
    n_ch = s // c
    f = lower_bound + (1.0 - lower_bound) * jax.nn.sigmoid(f_cols.astype(jnp.float32))
    log_f = jnp.log(f)
    k = 1.0 - f

    def chunks(a):
        return a.astype(jnp.float32).reshape(b, n_ch, c, h, dk).transpose(1, 0, 3, 2, 4)

    causal = jnp.tril(jnp.ones((c, c), dtype=bool))

    def step(state, inp):
        qc, kc, vc, lf = inp
        gcum = jnp.cumsum(lf, axis=2)
        diff = jnp.where(causal[:, :, None], gcum[:, :, :, None, :] - gcum[:, :, None, :, :], -jnp.inf)
        scores = jnp.einsum('bhtsk,bhsk->bhts', qc[:, :, :, None, :] * jnp.exp(diff), kc)
        o = (jnp.einsum('bhts,bhsv->bhtv', scores, vc)
             + jnp.einsum('bhtk,bhkv->bhtv', qc * jnp.exp(gcum), state))
        g_last = gcum[:, :, -1]
        state = (jnp.exp(g_last)[..., None] * state
                 + jnp.einsum('bhsk,bhsv->bhkv', kc * jnp.exp(g_last[:, :, None] - gcum), vc))
        return state, o

    s_init = jnp.zeros((b, h, dk, dk), jnp.float32)
    _, o = lax.scan(step, s_init, (chunks(q_cols), chunks(k), chunks(i_cols), chunks(log_f)))
    o = o.transpose(1, 0, 3, 2, 4).reshape(b, s, h, dk)
    o = rms_norm(o, norm_g).reshape(b, s, HG_WIDTH)
    return o * jax.nn.silu(g_cols.astype(jnp.float32))


def ssd_chunked(x, a, bm, cm):
    b, s, h, p = x.shape
    g, n = bm.shape[2], bm.shape[3]
    r, q = h // g, M2_CHUNK
    c = s // q
    x = x.reshape(b, c, q, g, r, p)
    a = a.reshape(b, c, q, g, r).transpose(0, 3, 4, 1, 2)
    bm = bm.reshape(b, c, q, g, n)
    cm = cm.reshape(b, c, q, g, n)
    a_cs = jnp.cumsum(a, axis=-1)
    lmat = jnp.exp(segsum(a))
    cb = jnp.einsum('bclgn,bcsgn->bgcls', cm, bm)
    y_diag = jnp.einsum('bgrcls,bcsgrp->bclgrp', cb[:, :, None] * lmat, x)
    decay_states = jnp.exp(a_cs[..., -1:] - a_cs).transpose(0, 3, 4, 1, 2)
    states = jnp.einsum('bclgn,bclgrp->bcgrpn', bm, x * decay_states[..., None])
    states = jnp.concatenate([jnp.zeros_like(states[:, :1]), states], axis=1)
    decay_chunk = jnp.exp(segsum(jnp.pad(a_cs[..., -1], ((0, 0), (0, 0), (0, 0), (1, 0)))))
    states = jnp.einsum('bgrzc,bcgrpn->bzgrpn', decay_chunk, states)[:, :-1]
    state_decay_out = jnp.exp(a_cs).transpose(0, 3, 4, 1, 2)
    y_off = jnp.einsum('bclgn,bcgrpn->bclgrp', cm, states) * state_decay_out[..., None]
    return (y_diag + y_off).reshape(b, s, h, p)


def mamba2_mixer(z, xbc, dt_cols, conv_w, conv_b, dt_bias, a_log, d_skip, norm_g):
    b, s, _ = z.shape
    xbc = jax.nn.silu(causal_dwconv(xbc, conv_w, conv_b))
    xs, bm, cm = split_cols(xbc, (M2_INNER, M2_GROUPS * M2_STATE, M2_GROUPS * M2_STATE))
    dt = jax.nn.softplus(dt_cols.astype(jnp.float32) + dt_bias)
    a = -jnp.exp(a_log.astype(jnp.float32))
    xh = xs.astype(jnp.float32).reshape(b, s, M2_HEADS, M2_HEAD_DIM)
    y = ssd_chunked(xh * dt[..., None], dt * a,
                    bm.astype(jnp.float32).reshape(b, s, M2_GROUPS, M2_STATE),
                    cm.astype(jnp.float32).reshape(b, s, M2_GROUPS, M2_STATE))
    y = y + d_skip[:, None] * xh
    y = y.reshape(b, s, M2_INNER) * jax.nn.silu(z.astype(jnp.float32))
    y = rms_norm(y.reshape(b, s, M2_GROUPS, M2_INNER // M2_GROUPS), norm_g.reshape(M2_GROUPS, -1))
    return y.reshape(b, s, M2_INNER)


def conv_ffn(h, w_up, conv_w, conv_b, w_down):
    u = causal_dwconv(h @ w_up, conv_w, conv_b)
    gate, up = jnp.split(u, 2, axis=-1)
    return (jax.nn.silu(gate) * up) @ w_down


def setup_inputs(seed: int = 0) -> dict:
    key = jax.random.key(seed)
    ks = jax.random.split(key, 32)
    f32 = jnp.float32
    lyr = DEPTH

    def nrm(k, shape, scale):
        return jax.random.normal(k, shape, f32) * scale

    cin = NSA_CMP_BLOCK * NSA_HEAD_DIM
    dt0 = jnp.exp(jax.random.uniform(ks[17], (lyr, M2_HEADS), f32, math.log(1e-3), math.log(1e-1)))
    return {
        'x': nrm(ks[0], (BATCH, SEQ, D_MODEL), 1.0),
        'positions': (jax.random.randint(ks[1], (BATCH, 1), 0, 1024) + jnp.arange(SEQ)[None, :]).astype(jnp.int32),
        'attn_norm_g': 1.0 + nrm(ks[2], (lyr, D_MODEL), 0.05),
        'ffn_norm_g': 1.0 + nrm(ks[3], (lyr, D_MODEL), 0.05),
        'w_in': nrm(ks[4], (lyr, D_MODEL, IN_DIM), D_MODEL ** -0.5),
        'nsa_q_norm_g': 1.0 + nrm(ks[5], (lyr, NSA_HEAD_DIM), 0.05),
        'nsa_k_norm_g': 1.0 + nrm(ks[6], (lyr, 3, NSA_HEAD_DIM), 0.05),
        'nsa_cmp_pos_k': nrm(ks[7], (lyr, NSA_CMP_BLOCK, NSA_HEAD_DIM), 0.5),
        'nsa_cmp_pos_v': nrm(ks[8], (lyr, NSA_CMP_BLOCK, NSA_HEAD_DIM), 0.5),
        'nsa_cmp_k_w1': nrm(ks[9], (lyr, cin, NSA_CMP_HIDDEN), cin ** -0.5),
        'nsa_cmp_k_w2': nrm(ks[10], (lyr, NSA_CMP_HIDDEN, NSA_HEAD_DIM), NSA_CMP_HIDDEN ** -0.5),
        'nsa_cmp_v_w1': nrm(ks[11], (lyr, cin, NSA_CMP_HIDDEN), cin ** -0.5),
        'nsa_cmp_v_w2': nrm(ks[12], (lyr, NSA_CMP_HIDDEN, NSA_HEAD_DIM), NSA_CMP_HIDDEN ** -0.5),
        'hgrn_lb_logits': nrm(ks[13], (lyr, HG_WIDTH), 1.0),
        'hgrn_norm_g': 1.0 + nrm(ks[14], (lyr, HG_HEAD_DIM), 0.05),
        'm2_conv_w': nrm(ks[15], (lyr, M2_CONV, M2_XBC), M2_CONV ** -0.5),
        'm2_conv_b': nrm(ks[16], (lyr, M2_XBC), 0.01),
        'm2_dt_bias': dt0 + jnp.log(-jnp.expm1(-dt0)),
        'm2_a_log': jnp.log(jax.random.uniform(ks[18], (lyr, M2_HEADS), f32, 1.0, 16.0)),
        'm2_d_skip': 1.0 + nrm(ks[19], (lyr, M2_HEADS), 0.1),
        'm2_norm_g': 1.0 + nrm(ks[20], (lyr, M2_INNER), 0.05),
        'w_branch_nsa': nrm(ks[21], (lyr, NSA_WIDTH, D_MODEL), NSA_WIDTH ** -0.5),
        'w_branch_hgrn': nrm(ks[22], (lyr, HG_WIDTH, D_MODEL), HG_WIDTH ** -0.5),
        'w_branch_m2': nrm(ks[23], (lyr, M2_INNER, D_MODEL), M2_INNER ** -0.5),
        'w_out': nrm(ks[24], (lyr, D_MODEL, D_MODEL), D_MODEL ** -0.5),
        'ffn_w_up': nrm(ks[25], (lyr, D_MODEL, 2 * FFN_DIM), D_MODEL ** -0.5),
        'ffn_conv_w': nrm(ks[26], (lyr, FFN_CONV, 2 * FFN_DIM), FFN_CONV ** -0.5),
        'ffn_conv_b': nrm(ks[27], (lyr, 2 * FFN_DIM), 0.01),
        'ffn_w_down': nrm(ks[28], (lyr, FFN_DIM, D_MODEL), FFN_DIM ** -0.5),
    }


def reference(x, positions, attn_norm_g, ffn_norm_g, w_in, nsa_q_norm_g, nsa_k_norm_g,
              nsa_cmp_pos_k, nsa_cmp_pos_v, nsa_cmp_k_w1, nsa_cmp_k_w2, nsa_cmp_v_w1, nsa_cmp_v_w2,
              hgrn_lb_logits, hgrn_norm_g, m2_conv_w, m2_conv_b, m2_dt_bias, m2_a_log, m2_d_skip,
              m2_norm_g, w_branch_nsa, w_branch_hgrn, w_branch_m2, w_out,
              ffn_w_up, ffn_conv_w, ffn_conv_b, ffn_w_down):
    lb_soft = jax.nn.softmax(hgrn_lb_logits.astype(jnp.float32), axis=0)
    lower_bounds = jnp.cumsum(lb_soft, axis=0) - lb_soft[0]
    for l in range(DEPTH):
        h = rms_norm(x, attn_norm_g[l])
        (nsa_q, nsa_kv, nsa_gate, hg_q, hg_f, hg_i, hg_g,
         m2_z, m2_xbc, m2_dt, merge_gate) = split_cols(h @ w_in[l], IN_SPLITS)
        y_a = nsa_mixer(nsa_q, nsa_kv, nsa_gate, positions, nsa_q_norm_g[l], nsa_k_norm_g[l],
                        nsa_cmp_pos_k[l], nsa_cmp_pos_v[l], nsa_cmp_k_w1[l], nsa_cmp_k_w2[l],
                        nsa_cmp_v_w1[l], nsa_cmp_v_w2[l])
        y_b = hgrn2_mixer(hg_q, hg_f, hg_i, hg_g, lower_bounds[l], hgrn_norm_g[l])
        y_c = mamba2_mixer(m2_z, m2_xbc, m2_dt, m2_conv_w[l], m2_conv_b[l], m2_dt_bias[l],
                           m2_a_log[l], m2_d_skip[l], m2_norm_g[l])
        g_a, g_b, g_c = jnp.split(jax.nn.sigmoid(merge_gate.astype(jnp.float32)), 3, axis=-1)
        merged = (g_a * (y_a @ w_branch_nsa[l]) + g_b * (y_b @ w_branch_hgrn[l])
                  + g_c * (y_c @ w_branch_m2[l]))
        x = x + (merged @ w_out[l]).astype(x.dtype)
        x = x + conv_ffn(rms_norm(x, ffn_norm_g[l]), ffn_w_up[l], ffn_conv_w[l], ffn_conv_b[l],
                         ffn_w_down[l]).astype(x.dtype)
    return x
```

```python
import functools
import math

import numpy as np
import jax
import jax.numpy as jnp
from jax import lax
from jax.experimental import pallas as pl
from jax.experimental.pallas import tpu as pltpu

F32 = jnp.float32
BF16 = jnp.bfloat16

D_MODEL = 1024
NSA_HEADS = 8
NSA_KV_HEADS = 2
NSA_REP = NSA_HEADS // NSA_KV_HEADS
NSA_HEAD_DIM = 64
NSA_CMP_BLOCK = 32
NSA_CMP_STRIDE = 16
NSA_SEL_BLOCK = 64
NSA_TOP_N = 16
NSA_WINDOW = 512
NSA_CMP_HIDDEN = 256
NSA_Q_BLOCK = 128
NSA_BIG = 1e9
ROPE_THETA = 500000.0
ROPE_DIM = NSA_HEAD_DIM // 4
NSA_WIDTH = NSA_HEADS * NSA_HEAD_DIM
NSA_KV_WIDTH = NSA_KV_HEADS * NSA_HEAD_DIM
HG_HEADS = 4
HG_HEAD_DIM = 128
HG_WIDTH = HG_HEADS * HG_HEAD_DIM
HG_SUB = 16
M2_HEADS = 16
M2_HEAD_DIM = 64
M2_INNER = M2_HEADS * M2_HEAD_DIM
M2_GROUPS = 2
M2_STATE = 128
M2_CONV = 4
M2_CHUNK = 128
M2_BC = M2_GROUPS * M2_STATE
FFN_DIM = 2816
FFN_CONV = 3
EPS = 1e-6

NEG = -1e30
LANES = 128
SUBLANES = 8

C_Z = 0
C_XS = 1024
C_GA = 2048
C_GB = 3072
C_GC = 4096
C_Q = 5120
C_HQ = 5632
C_HF = 6144
C_HI = 6656
C_HG = 7168
C_B = 7680
C_C = 7936
C_KV = 8192
C_SMALL = 8960
P_DIM = 9216
SMALL_DT = 3 * NSA_HEADS

_SRC = np.cumsum([0, NSA_WIDTH, 6 * NSA_KV_WIDTH, 3 * NSA_HEADS, HG_WIDTH, HG_WIDTH, HG_WIDTH, HG_WIDTH,
                  M2_INNER, M2_INNER + 2 * M2_BC, M2_HEADS, 3 * D_MODEL]).tolist()


def _cparams(sem, vmem_mib):
    return pltpu.CompilerParams(dimension_semantics=sem, vmem_limit_bytes=vmem_mib * 1024 * 1024)


def _sigmoid(x):
    return 1.0 / (1.0 + jnp.exp(-x))


def _silu(x):
    return x * _sigmoid(x)


def _dot(a, b):
    return jnp.dot(a, b, preferred_element_type=F32)


def _dot_nt(a, b):
    return lax.dot_general(a, b, (((1,), (1,)), ((), ())), preferred_element_type=F32)


def _dot_tn(a, b):
    return lax.dot_general(a, b, (((0,), (0,)), ((), ())), preferred_element_type=F32)


def _dot_hi(a, b):
    return jnp.dot(a, b, preferred_element_type=F32, precision=lax.Precision.HIGHEST)


def _pack_w_in(w):
    o = _SRC
    xbc = o[8]
    pieces = [
        w[:, o[7]:o[8]],
        w[:, xbc:xbc + M2_INNER],
        w[:, o[10]:o[10] + D_MODEL],
        w[:, o[10] + D_MODEL:o[10] + 2 * D_MODEL],
        w[:, o[10] + 2 * D_MODEL:o[11]],
        w[:, o[0]:o[1]],
        w[:, o[3]:o[4]], w[:, o[4]:o[5]], w[:, o[5]:o[6]], w[:, o[6]:o[7]],
        w[:, xbc + M2_INNER:xbc + M2_INNER + M2_BC],
        w[:, xbc + M2_INNER + M2_BC:o[9]],
        w[:, o[1]:o[2]],
        w[:, o[2]:o[3]],
        w[:, o[9]:o[10]],
    ]
    packed = jnp.concatenate(pieces, axis=1)
    packed = jnp.pad(packed, ((0, 0), (0, P_DIM - packed.shape[1])))
    return packed.astype(BF16)


def _inproj_body(x_ref, g_ref, w_ref, o_ref, h_ref):
    @pl.when(pl.program_id(1) == 0)
    def _():
        x = x_ref[...]
        ms = jnp.mean(x * x, axis=-1, keepdims=True)
        h_ref[...] = (x * lax.rsqrt(ms + EPS) * g_ref[...]).astype(BF16)

    o_ref[...] = _dot(h_ref[...], w_ref[...])


def _inproj(x2, g, w):
    n = x2.shape[0]
    tm = 1024 if n % 1024 == 0 else 512
    tn = 1024
    return pl.pallas_call(
        _inproj_body,
        name="inproj",
        grid=(n // tm, P_DIM // tn),
        in_specs=[pl.BlockSpec((tm, D_MODEL), lambda i, j: (i, 0)),
                  pl.BlockSpec((1, D_MODEL), lambda i, j: (0, 0)),
                  pl.BlockSpec((D_MODEL, tn), lambda i, j: (0, j))],
        out_specs=pl.BlockSpec((tm, tn), lambda i, j: (i, j)),
        out_shape=jax.ShapeDtypeStruct((n, P_DIM), F32),
        scratch_shapes=[pltpu.VMEM((tm, D_MODEL), BF16)],
        compiler_params=_cparams(("parallel", "arbitrary"), 48),
    )(x2, g, w)


def _head_sumsq(x, bd_ref):
    return _dot_hi(x * x, bd_ref[...])


def _rope(y, cos_t, sa_t, sb_t):
    w = y.shape[-1]
    return y * cos_t + pltpu.roll(y, w - ROPE_DIM // 2, 1) * sa_t + pltpu.roll(y, ROPE_DIM // 2, 1) * sb_t


def _nsaprep_body(q_ref, ksel_ref, vsel_ref, kwin_ref, vwin_ref, small_ref, ang_ref,
                  bdq_ref, bdk_ref, qg_ref, kg_ref,
                  qn_o, qr_o, ks_o, vs_o, kw_o, vw_o, gs_o):
    dh = NSA_HEAD_DIM
    half = ROPE_DIM // 2
    ang = ang_ref[...]
    lane = lax.broadcasted_iota(jnp.int32, ang.shape, 1) % dh
    cos_t = jnp.cos(ang)
    sin_t = jnp.sin(ang)
    sa_t = jnp.where(lane < half, -sin_t, 0.0)
    sb_t = jnp.where((lane >= half) & (lane < ROPE_DIM), sin_t, 0.0)
    rep = NSA_WIDTH // LANES
    cos_q = jnp.concatenate([cos_t] * rep, axis=1)
    sa_q = jnp.concatenate([sa_t] * rep, axis=1)
    sb_q = jnp.concatenate([sb_t] * rep, axis=1)

    scale = dh ** -0.5
    q = q_ref[...]
    qn = q * lax.rsqrt(_head_sumsq(q, bdq_ref) * (1.0 / dh) + EPS) * qg_ref[...]
    qr = _rope(qn, cos_q, sa_q, sb_q)
    qn_s = (qn * scale).astype(BF16)
    qr_s = (qr * scale).astype(BF16)
    for h in range(NSA_HEADS):
        qn_o[0, h] = qn_s[:, h * dh:(h + 1) * dh]
        qr_o[0, h] = qr_s[:, h * dh:(h + 1) * dh]

    def knorm(k_ref, row):
        k = k_ref[...]
        kn = k * lax.rsqrt(_head_sumsq(k, bdk_ref) * (1.0 / dh) + EPS) * kg_ref[row:row + 1, :]
        return _rope(kn, cos_t, sa_t, sb_t).astype(BF16)

    ks = knorm(ksel_ref, 1)
    kw = knorm(kwin_ref, 2)
    vs = vsel_ref[...].astype(BF16)
    vw = vwin_ref[...].astype(BF16)
    for g in range(NSA_KV_HEADS):
        ks_o[0, g] = ks[:, g * dh:(g + 1) * dh]
        kw_o[0, g] = kw[:, g * dh:(g + 1) * dh]
        vs_o[0, g] = vs[:, g * dh:(g + 1) * dh]
        vw_o[0, g] = vw[:, g * dh:(g + 1) * dh]

    sg = _sigmoid(small_ref[...])
    gs_o[0, 0] = sg
    gs_o[0, 1] = pltpu.roll(sg, LANES - 3 * NSA_REP, 1)


def _nsaprep(p, ang, qg, kg, b, s):
    t = 512
    nt = s // t
    dh = NSA_HEAD_DIM
    bdq = jnp.asarray(np.kron(np.eye(NSA_HEADS), np.ones((dh, dh))), F32)
    bdk = jnp.asarray(np.kron(np.eye(NSA_KV_HEADS), np.ones((dh, dh))), F32)
    qg_t = jnp.tile(qg, NSA_HEADS)[None, :]
    kg_t = jnp.tile(kg, (1, NSA_KV_HEADS))
    kvb = C_KV // LANES

    def col(width, cb):
        return pl.BlockSpec((t, width), lambda bi, i: (bi * nt + i, cb))

    def full(shape):
        return pl.BlockSpec(shape, lambda bi, i: (0,) * len(shape))

    hm = lambda heads: pl.BlockSpec((1, heads, t, dh), lambda bi, i: (bi, 0, i, 0))
    out_shape = (
        jax.ShapeDtypeStruct((b, NSA_HEADS, s, dh), BF16),
        jax.ShapeDtypeStruct((b, NSA_HEADS, s, dh), BF16),
        jax.ShapeDtypeStruct((b, NSA_KV_HEADS, s, dh), BF16),
        jax.ShapeDtypeStruct((b, NSA_KV_HEADS, s, dh), BF16),
        jax.ShapeDtypeStruct((b, NSA_KV_HEADS, s, dh), BF16),
        jax.ShapeDtypeStruct((b, NSA_KV_HEADS, s, dh), BF16),
        jax.ShapeDtypeStruct((b, NSA_KV_HEADS, s, LANES), F32),
    )
    return pl.pallas_call(
        _nsaprep_body,
        name="nsaprep",
        grid=(b, nt),
        in_specs=[col(NSA_WIDTH, C_Q // NSA_WIDTH),
                  col(LANES, kvb + 2), col(LANES, kvb + 3), col(LANES, kvb + 4), col(LANES, kvb + 5),
                  col(LANES, C_SMALL // LANES),
                  col(LANES, 0),
                  full((NSA_WIDTH, NSA_WIDTH)), full((LANES, LANES)),
                  full((1, NSA_WIDTH)), full((3, LANES))],
        out_specs=(hm(NSA_HEADS), hm(NSA_HEADS), hm(NSA_KV_HEADS), hm(NSA_KV_HEADS),
                   hm(NSA_KV_HEADS), hm(NSA_KV_HEADS),
                   pl.BlockSpec((1, NSA_KV_HEADS, t, LANES), lambda bi, i: (bi, 0, i, 0))),
        out_shape=out_shape,
        compiler_params=_cparams(("parallel", "parallel"), 40),
    )(p, p, p, p, p, p, ang, bdq, bdk, qg_t, kg_t)


def _compress_body(rk_ref, rv_ref, pk_ref, pv_ref, kw1_ref, kw2_ref, vw1_ref, vw2_ref, kg_ref,
                   kc_o, vc_o):
    half = NSA_CMP_STRIDE * NSA_HEAD_DIM

    def mlp(r_ref, p_ref, w1_ref, w2_ref):
        r = r_ref[0, 0]
        nc = r.shape[0]
        top = _dot(r, w1_ref[0:half, :])
        bot = _dot(r, w1_ref[half:2 * half, :])
        posb = _dot(p_ref[...], w1_ref[...])[0:1, :]
        hid = top + pltpu.roll(bot, nc - 1, 0) + posb
        act = jax.nn.gelu(hid, approximate=True)
        return _dot(act.astype(BF16), w2_ref[...])

    kc = mlp(rk_ref, pk_ref, kw1_ref, kw2_ref)
    ms = jnp.mean(kc * kc, axis=-1, keepdims=True)
    kc_o[0, 0] = (kc * lax.rsqrt(ms + EPS) * kg_ref[...]).astype(BF16)
    vc_o[0, 0] = mlp(rv_ref, pv_ref, vw1_ref, vw2_ref).astype(BF16)


def _compress(p, pos_k, pos_v, kw1, kw2, vw1, vw2, kg0, b, s):
    dh = NSA_HEAD_DIM
    g = NSA_KV_HEADS
    nc = s // NSA_CMP_STRIDE
    width = NSA_CMP_STRIDE * dh
    raw = p[:, C_KV:C_KV + 2 * LANES].reshape(b, nc, NSA_CMP_STRIDE, 2, g, dh)
    raw = raw.transpose(3, 0, 4, 1, 2, 5).reshape(2, b, g, nc, width).astype(BF16)

    def pos_rows(pe):
        return jnp.broadcast_to(pe.reshape(1, NSA_CMP_BLOCK * dh), (SUBLANES, NSA_CMP_BLOCK * dh)).astype(BF16)

    blk = pl.BlockSpec((1, 1, nc, width), lambda bi, gi: (bi, gi, 0, 0))

    def full(shape):
        return pl.BlockSpec(shape, lambda bi, gi: (0,) * len(shape))

    oblk = pl.BlockSpec((1, 1, nc, dh), lambda bi, gi: (bi, gi, 0, 0))
    return pl.pallas_call(
        _compress_body,
        name="nsacompress",
        grid=(b, g),
        in_specs=[blk, blk,
                  full((SUBLANES, 2 * width)), full((SUBLANES, 2 * width)),
                  full((2 * width, NSA_CMP_HIDDEN)), full((NSA_CMP_HIDDEN, dh)),
                  full((2 * width, NSA_CMP_HIDDEN)), full((NSA_CMP_HIDDEN, dh)),
                  full((1, dh))],
        out_specs=(oblk, oblk),
        out_shape=(jax.ShapeDtypeStruct((b, g, nc, dh), BF16),
                   jax.ShapeDtypeStruct((b, g, nc, dh), BF16)),
        compiler_params=_cparams(("parallel", "parallel"), 40),
    )(raw[0], raw[1], pos_rows(pos_k), pos_rows(pos_v),
      kw1.astype(BF16), kw2.astype(BF16), vw1.astype(BF16), vw2.astype(BF16), kg0[None, :])


NSA_KT = 512


def _nsa_body(qn_ref, qr_ref, kc_ref, vc_ref, ks_ref, vs_ref, kw_ref, vw_ref, gs_ref, c2s_ref, eexp_ref,
              o_ref, *, top_n):
    tq = NSA_Q_BLOCK
    rep = NSA_REP
    dh = NSA_HEAD_DIM
    rows = rep * tq
    s0 = pl.program_id(2) * tq
    qn = qn_ref[0].reshape(rows, dh)
    qr = qr_ref[0].reshape(rows, dh)
    t_col = s0 + (lax.broadcasted_iota(jnp.int32, (rows, 1), 0) & (tq - 1))

    nc = kc_ref.shape[2]
    sc = _dot_nt(qn, kc_ref[0, 0])
    cj = lax.broadcasted_iota(jnp.int32, (1, nc), 1)
    cmask = (cj * NSA_CMP_STRIDE + (NSA_CMP_BLOCK - 1)) <= t_col
    sc = jnp.where(cmask, sc, NEG)
    m = jnp.max(sc, axis=-1, keepdims=True)
    e = jnp.where(cmask, jnp.exp(sc - m), 0.0)
    inv = 1.0 / jnp.maximum(jnp.sum(e, axis=-1, keepdims=True), 1e-30)
    p_cmp = e * inv
    o_cmp = _dot(p_cmp.astype(BF16), vc_ref[0, 0])

    psum = p_cmp[0:tq]
    for r in range(1, rep):
        psum = psum + p_cmp[r * tq:(r + 1) * tq]
    imp = _dot_hi(psum, c2s_ref[...])
    nsp = imp.shape[1]
    n_sel = ks_ref.shape[2] // NSA_SEL_BLOCK
    t_q = s0 + lax.broadcasted_iota(jnp.int32, (tq, 1), 0)
    nb = lax.broadcasted_iota(jnp.int32, (1, nsp), 1)
    cur = t_q // NSA_SEL_BLOCK
    forced = (nb == 0) | (nb == cur) | (nb == cur - 1)
    valid = nb * NSA_SEL_BLOCK <= t_q
    work = jnp.where(forced, NSA_BIG, jnp.where(valid, imp, -NSA_BIG))
    work = jnp.where(nb < n_sel, work, -jnp.inf)
    nbf = nb.astype(F32)
    sel = jnp.zeros((tq, nsp), F32)
    for _ in range(top_n):
        mx = jnp.max(work, axis=-1, keepdims=True)
        idx = jnp.min(jnp.where(work == mx, nbf, float(nsp)), axis=-1, keepdims=True)
        hit = nbf == idx
        sel = jnp.where(hit, 1.0, sel)
        work = jnp.where(hit, -jnp.inf, work)
    selb = sel.astype(BF16)

    kt = NSA_KT
    kl = lax.broadcasted_iota(jnp.int32, (1, kt), 1)

    def sel_tile(j, carry):
        m_i, l_i, acc = carry
        k0 = pl.multiple_of(j * kt, kt)
        k = ks_ref[0, 0, pl.ds(k0, kt), :]
        v = vs_ref[0, 0, pl.ds(k0, kt), :]
        s = _dot_nt(qr, k)
        ex = _dot(selb, eexp_ref[:, pl.ds(k0, kt)])
        ex = jnp.concatenate([ex] * rep, axis=0)
        mask = (ex > 0.5) & ((k0 + kl) <= t_col)
        s = jnp.where(mask, s, NEG)
        m_new = jnp.maximum(m_i, jnp.max(s, axis=-1, keepdims=True))
        alpha = jnp.exp(m_i - m_new)
        pe = jnp.where(mask, jnp.exp(s - m_new), 0.0)
        l_new = alpha * l_i + jnp.sum(pe, axis=-1, keepdims=True)
        acc = alpha * acc + _dot(pe.astype(BF16), v)
        return m_new, l_new, acc

    n_kt = (s0 + tq + kt - 1) // kt
    init = (jnp.full((rows, 1), NEG, F32), jnp.zeros((rows, 1), F32), jnp.zeros((rows, dh), F32))
    _, l_s, acc_s = lax.fori_loop(0, n_kt, sel_tile, init)
    o_sel = acc_s * (1.0 / jnp.maximum(l_s, 1e-30))

    wlen = NSA_WINDOW + tq
    kstart = pl.multiple_of(jnp.maximum(s0 - NSA_WINDOW, 0), tq)
    kwin = kw_ref[0, 0, pl.ds(kstart, wlen), :]
    vwin = vw_ref[0, 0, pl.ds(kstart, wlen), :]
    sw = _dot_nt(qr, kwin)
    kpos = kstart + lax.broadcasted_iota(jnp.int32, (1, wlen), 1)
    wmask = (kpos <= t_col) & ((t_col - kpos) < NSA_WINDOW)
    sw = jnp.where(wmask, sw, NEG)
    mw = jnp.max(sw, axis=-1, keepdims=True)
    ew = jnp.where(wmask, jnp.exp(sw - mw), 0.0)
    lw = jnp.sum(ew, axis=-1, keepdims=True)
    o_win = _dot(ew.astype(BF16), vwin) * (1.0 / jnp.maximum(lw, 1e-30))

    gs = gs_ref[0, 0]

    def gate(c):
        return jnp.concatenate([gs[:, 3 * r + c:3 * r + c + 1] for r in range(rep)], axis=0)

    o = gate(0) * o_cmp + gate(1) * o_sel + gate(2) * o_win
    o_ref[...] = jnp.concatenate([o[r * tq:(r + 1) * tq] for r in range(rep)], axis=1)


def _nsa_attention(qn, qr, kc, vc, ks, vs, kw, vw, gs, b, s):
    tq = NSA_Q_BLOCK
    dh = NSA_HEAD_DIM
    g = NSA_KV_HEADS
    nq = s // tq
    nc = s // NSA_CMP_STRIDE
    n_sel = s // NSA_SEL_BLOCK
    nsp = -(-n_sel // LANES) * LANES
    top_n = min(NSA_TOP_N, n_sel)
    c_start = np.arange(nc) * NSA_CMP_STRIDE
    s_start = np.arange(nsp) * NSA_SEL_BLOCK
    overlap = np.clip(np.minimum(c_start[:, None] + NSA_CMP_BLOCK, s_start[None, :] + NSA_SEL_BLOCK)
                      - np.maximum(c_start[:, None], s_start[None, :]), 0, None)
    c2s = jnp.asarray(overlap / NSA_CMP_BLOCK, F32)
    eexp = jnp.asarray(np.arange(nsp)[:, None] == (np.arange(s) // NSA_SEL_BLOCK)[None, :], BF16)

    qblk = pl.BlockSpec((1, NSA_REP, tq, dh), lambda bi, gi, i: (bi, gi, i, 0))

    def whole(n):
        return pl.BlockSpec((1, 1, n, dh), lambda bi, gi, i: (bi, gi, 0, 0))

    def full(shape):
        return pl.BlockSpec(shape, lambda bi, gi, i: (0,) * len(shape))

    return pl.pallas_call(
        functools.partial(_nsa_body, top_n=top_n),
        name="nsaattn",
        grid=(b, g, nq),
        in_specs=[qblk, qblk, whole(nc), whole(nc), whole(s), whole(s), whole(s), whole(s),
                  pl.BlockSpec((1, 1, tq, LANES), lambda bi, gi, i: (bi, gi, i, 0)),
                  full((nc, nsp)), full((nsp, s))],
        out_specs=pl.BlockSpec((tq, NSA_REP * dh), lambda bi, gi, i: (bi * nq + i, gi)),
        out_shape=jax.ShapeDtypeStruct((b * s, NSA_WIDTH), F32),
        compiler_params=_cparams(("parallel", "parallel", "arbitrary"), 56),
    )(qn, qr, kc, vc, ks, vs, kw, vw, gs, c2s, eexp)


HG_TILE = 256


def _hgrn_body(q_ref, f_ref, i_ref, g_ref, lbl_ref, ng_ref, tri_ref, o_ref,
               st_ref, gc_ref, k_ref, oacc_ref, *, layer):
    sub = HG_SUB
    dk = HG_HEAD_DIM
    th = q_ref.shape[0]

    @pl.when(pl.program_id(1) == 0)
    def _():
        st_ref[...] = jnp.zeros_like(st_ref)

    lbl = lbl_ref[...]
    el = jnp.exp(lbl - jnp.max(lbl, axis=0, keepdims=True))
    soft = el / jnp.sum(el, axis=0, keepdims=True)
    lb = jnp.zeros_like(soft[0:1])
    for d in range(1, layer + 1):
        lb = lb + soft[d:d + 1]

    f = lb + (1.0 - lb) * _sigmoid(f_ref[...])
    lf = jnp.log(f)
    k_ref[...] = 1.0 - f
    blk = tri_ref.shape[0]
    for c in range(th // blk):
        gc_ref[c * blk:(c + 1) * blk, :] = _dot_hi(tri_ref[...], lf[c * blk:(c + 1) * blk, :])

    srow = lax.broadcasted_iota(jnp.int32, (sub, 1), 0)

    def step(c, carry):
        r0 = pl.multiple_of(c * sub, sub)
        gall = gc_ref[pl.ds(r0, sub), :]
        qall = q_ref[pl.ds(r0, sub), :]
        kall = k_ref[pl.ds(r0, sub), :]
        vall = i_ref[pl.ds(r0, sub), :]
        outs = []
        for h in range(HG_HEADS):
            cs = slice(h * dk, (h + 1) * dk)
            g, q, kk, v = gall[:, cs], qall[:, cs], kall[:, cs], vall[:, cs]
            st = st_ref[h]
            o = _dot_nt((q * jnp.exp(g)).astype(BF16), st.astype(BF16))
            for t in range(sub):
                d = jnp.where(srow <= t, g[t:t + 1, :] - g, NEG)
                w = (q[t:t + 1, :] * kk) * jnp.exp(d)
                r = jnp.sum(w, axis=-1, keepdims=True)
                ot = jnp.sum(r * v, axis=0, keepdims=True)
                o = o + jnp.where(srow == t, ot, 0.0)
            g_last = g[sub - 1:sub, :]
            kt = kk * jnp.exp(g_last - g)
            st_ref[h] = st * jnp.exp(g_last) + _dot_tn(v.astype(BF16), kt.astype(BF16))
            outs.append(o)
        oacc_ref[pl.ds(r0, sub), :] = jnp.concatenate(outs, axis=1)
        return carry

    lax.fori_loop(0, th // sub, step, 0)

    o = oacc_ref[...]
    parts = []
    for h in range(HG_HEADS):
        oh = o[:, h * dk:(h + 1) * dk]
        ms = jnp.mean(oh * oh, axis=-1, keepdims=True)
        parts.append(oh * lax.rsqrt(ms + EPS) * ng_ref[...])
    o_ref[...] = jnp.concatenate(parts, axis=1) * _silu(g_ref[...])


def _hgrn(p, lb_logits, norm_g, layer, b, s):
    th = HG_TILE
    nt = s // th
    blk = 64
    tri = jnp.asarray(np.kron(np.eye(blk // HG_SUB), np.tril(np.ones((HG_SUB, HG_SUB)))), F32)
    depth = lb_logits.shape[0]

    def col(cb):
        return pl.BlockSpec((th, HG_WIDTH), lambda bi, i: (bi * nt + i, cb))

    def full(shape):
        return pl.BlockSpec(shape, lambda bi, i: (0,) * len(shape))

    return pl.pallas_call(
        functools.partial(_hgrn_body, layer=layer),
        name="hgrn",
        grid=(b, nt),
        in_specs=[col(C_HQ // HG_WIDTH), col(C_HF // HG_WIDTH), col(C_HI // HG_WIDTH), col(C_HG // HG_WIDTH),
                  full((depth, HG_WIDTH)), full((1, HG_HEAD_DIM)), full((blk, blk))],
        out_specs=pl.BlockSpec((th, HG_WIDTH), lambda bi, i: (bi * nt + i, 0)),
        out_shape=jax.ShapeDtypeStruct((b * s, HG_WIDTH), F32),
        scratch_shapes=[pltpu.VMEM((HG_HEADS, HG_HEAD_DIM, HG_HEAD_DIM), F32),
                        pltpu.VMEM((th, HG_WIDTH), F32),
                        pltpu.VMEM((th, HG_WIDTH), F32),
                        pltpu.VMEM((th, HG_WIDTH), F32)],
        compiler_params=_cparams(("parallel", "arbitrary"), 40),
    )(p, p, p, p, lb_logits, norm_g[None, :], tri)


def _m2_body(z_ref, xs_ref, b_ref, c_ref, small_ref, cwx_ref, cwbc_ref, cbx_ref, cbbc_ref,
             dtb_ref, alog_ref, dskip_ref, ng_ref, tri_ref, trit_ref, e16_ref, o_ref,
             st_ref, px_ref, pbc_ref):
    ch = M2_CHUNK
    hp = M2_HEAD_DIM
    ns = M2_STATE
    halo = SUBLANES

    @pl.when(pl.program_id(1) == 0)
    def _():
        st_ref[...] = jnp.zeros_like(st_ref)
        px_ref[0:halo, :] = jnp.zeros((halo, M2_INNER), F32)
        pbc_ref[0:halo, :] = jnp.zeros((halo, 2 * M2_BC), F32)

    px_ref[halo:halo + ch, :] = xs_ref[...]
    pbc_ref[halo:halo + ch, 0:M2_BC] = b_ref[...]
    pbc_ref[halo:halo + ch, M2_BC:2 * M2_BC] = c_ref[...]

    def conv(p_ref, w_ref, bias_ref):
        acc = bias_ref[...]
        for k in range(M2_CONV):
            off = halo - (M2_CONV - 1) + k
            acc = acc + w_ref[k:k + 1, :] * p_ref[off:off + ch, :]
        return _silu(acc)

    xs = conv(px_ref, cwx_ref, cbx_ref)
    bc = conv(pbc_ref, cwbc_ref, cbbc_ref)
    px_ref[0:halo, :] = px_ref[ch:ch + halo, :]
    pbc_ref[0:halo, :] = pbc_ref[ch:ch + halo, :]

    dtr = small_ref[...] + dtb_ref[...]
    dt = jnp.maximum(dtr, 0.0) + jnp.log(1.0 + jnp.exp(-jnp.abs(dtr)))
    a = dt * (-jnp.exp(alog_ref[...]))
    a_cs = _dot_hi(tri_ref[...], a)
    a_cs_t = lax.dot_general(a, trit_ref[...], (((0,), (0,)), ((), ())),
                             preferred_element_type=F32, precision=lax.Precision.HIGHEST)
    a_last = a_cs[ch - 1:ch, :]
    e16 = e16_ref[...]
    dt_x = _dot_hi(dt, e16)
    dec_out_x = _dot_hi(jnp.exp(a_cs), e16)
    dec_st_x = _dot_hi(jnp.exp(a_last - a_cs), e16)
    xdt = xs * dt_x
    xdec = (xdt * dec_st_x).astype(BF16)
    xdt_b = xdt.astype(BF16)

    li = lax.broadcasted_iota(jnp.int32, (ch, ch), 0)
    si = lax.broadcasted_iota(jnp.int32, (ch, ch), 1)
    tril = li >= si
    ys = []
    hpg = M2_HEADS // M2_GROUPS
    for g in range(M2_GROUPS):
        bm = bc[:, g * ns:(g + 1) * ns].astype(BF16)
        cm = bc[:, M2_BC + g * ns:M2_BC + (g + 1) * ns].astype(BF16)
        cb = _dot_nt(cm, bm)
        for hh in range(hpg):
            h = g * hpg + hh
            hs = slice(h * hp, (h + 1) * hp)
            hl = SMALL_DT + h
            seg = jnp.where(tril, a_cs[:, hl:hl + 1] - a_cs_t[hl:hl + 1, :], NEG)
            y = _dot((cb * jnp.exp(seg)).astype(BF16), xdt_b[:, hs])
            st = st_ref[h]
            y = y + _dot_nt(cm, st.astype(BF16)) * dec_out_x[:, hs]
            st_ref[h] = st * jnp.exp(a_last[:, hl:hl + 1]) + _dot_tn(xdec[:, hs], bm)
            ys.append(y)
    y = jnp.concatenate(ys, axis=1) + dskip_ref[...] * xs
    y = y * _silu(z_ref[...])
    gw = M2_INNER // M2_GROUPS
    parts = []
    for g in range(M2_GROUPS):
        yg = y[:, g * gw:(g + 1) * gw]
        ms = jnp.mean(yg * yg, axis=-1, keepdims=True)
        parts.append(yg * lax.rsqrt(ms + EPS))
    o_ref[...] = jnp.concatenate(parts, axis=1) * ng_ref[...]


def _mamba2(p, conv_w, conv_b, dt_bias, a_log, d_skip, norm_g, b, s):
    ch = M2_CHUNK
    nt = s // ch
    tri = jnp.asarray(np.tril(np.ones((ch, ch))), F32)
    spread = np.zeros((LANES, M2_INNER))
    spread[SMALL_DT:SMALL_DT + M2_HEADS] = np.kron(np.eye(M2_HEADS), np.ones((1, M2_HEAD_DIM)))
    e16 = jnp.asarray(spread, F32)

    def lanes(v):
        return jnp.pad(v, (SMALL_DT, LANES - SMALL_DT - M2_HEADS))[None, :]

    def col(width, cb):
        return pl.BlockSpec((ch, width), lambda bi, i: (bi * nt + i, cb))

    def full(shape):
        return pl.BlockSpec(shape, lambda bi, i: (0,) * len(shape))

    return pl.pallas_call(
        _m2_body,
        name="mamba",
        grid=(b, nt),
        in_specs=[col(M2_INNER, C_Z // M2_INNER), col(M2_INNER, C_XS // M2_INNER),
                  col(M2_BC, C_B // M2_BC), col(M2_BC, C_C // M2_BC), col(LANES, C_SMALL // LANES),
                  full((M2_CONV, M2_INNER)), full((M2_CONV, 2 * M2_BC)),
                  full((1, M2_INNER)), full((1, 2 * M2_BC)),
                  full((1, LANES)), full((1, LANES)), full((1, M2_INNER)), full((1, M2_INNER)),
                  full((ch, ch)), full((ch, ch)), full((LANES, M2_INNER))],
        out_specs=pl.BlockSpec((ch, M2_INNER), lambda bi, i: (bi * nt + i, 0)),
        out_shape=jax.ShapeDtypeStruct((b * s, M2_INNER), F32),
        scratch_shapes=[pltpu.VMEM((M2_HEADS, M2_HEAD_DIM, M2_STATE), F32),
                        pltpu.VMEM((ch + SUBLANES, M2_INNER), F32),
                        pltpu.VMEM((ch + SUBLANES, 2 * M2_BC), F32)],
        compiler_params=_cparams(("parallel", "arbitrary"), 40),
    )(p, p, p, p, p,
      conv_w[:, :M2_INNER], conv_w[:, M2_INNER:], conv_b[None, :M2_INNER], conv_b[None, M2_INNER:],
      lanes(dt_bias), lanes(a_log), jnp.repeat(d_skip, M2_HEAD_DIM)[None, :], norm_g[None, :], tri, tri.T, e16)


def _merge_body(x_ref, ya_ref, yb_ref, yc_ref, ga_ref, gb_ref, gc_ref, wa_ref, wb_ref, wc_ref, wo_ref, o_ref):
    merged = (_sigmoid(ga_ref[...]) * _dot(ya_ref[...].astype(BF16), wa_ref[...])
              + _sigmoid(gb_ref[...]) * _dot(yb_ref[...].astype(BF16), wb_ref[...])
              + _sigmoid(gc_ref[...]) * _dot(yc_ref[...].astype(BF16), wc_ref[...]))
    o_ref[...] = x_ref[...] + _dot(merged.astype(BF16), wo_ref[...])


def _merge(x2, ya, yb, yc, p, wa, wb, wc, wo):
    n = x2.shape[0]
    tm = 256

    def rows(width, cb=0):
        return pl.BlockSpec((tm, width), lambda i: (i, cb))

    def full(shape):
        return pl.BlockSpec(shape, lambda i: (0,) * len(shape))

    return pl.pallas_call(
        _merge_body,
        name="merge",
        grid=(n // tm,),
        in_specs=[rows(D_MODEL), rows(NSA_WIDTH), rows(HG_WIDTH), rows(M2_INNER),
                  rows(D_MODEL, C_GA // D_MODEL), rows(D_MODEL, C_GB // D_MODEL), rows(D_MODEL, C_GC // D_MODEL),
                  full((NSA_WIDTH, D_MODEL)), full((HG_WIDTH, D_MODEL)), full((M2_INNER, D_MODEL)),
                  full((D_MODEL, D_MODEL))],
        out_specs=rows(D_MODEL),
        out_shape=jax.ShapeDtypeStruct((n, D_MODEL), F32),
        compiler_params=_cparams(("parallel",), 48),
    )(x2, ya, yb, yc, p, p, p, wa.astype(BF16), wb.astype(BF16), wc.astype(BF16), wo.astype(BF16))


FFN_FT = 1408


def _ffn_body(x_ref, xh_ref, g_ref, wg_ref, wu_ref, cwg_ref, cwu_ref, cbg_ref, cbu_ref, wd_ref, o_ref,
              h_ref, ug_ref, uu_ref, acc_ref, *, tiles_per_seq):
    halo = SUBLANES
    tm = x_ref.shape[0]
    j = pl.program_id(1)

    def norm(x):
        ms = jnp.mean(x * x, axis=-1, keepdims=True)
        return (x * lax.rsqrt(ms + EPS) * g_ref[...]).astype(BF16)

    @pl.when(j == 0)
    def _():
        first = (pl.program_id(0) % tiles_per_seq) == 0
        h_ref[0:halo, :] = jnp.where(first, jnp.zeros((halo, D_MODEL), BF16), norm(xh_ref[...]))
        h_ref[halo:halo + tm, :] = norm(x_ref[...])
        acc_ref[...] = jnp.zeros_like(acc_ref)

    h = h_ref[...]
    ug_ref[...] = _dot(h, wg_ref[...])
    uu_ref[...] = _dot(h, wu_ref[...])

    def conv(u_ref, w_ref, bias_ref):
        acc = bias_ref[...]
        for k in range(FFN_CONV):
            off = halo - (FFN_CONV - 1) + k
            acc = acc + w_ref[k:k + 1, :] * u_ref[off:off + tm, :]
        return acc

    act = _silu(conv(ug_ref, cwg_ref, cbg_ref)) * conv(uu_ref, cwu_ref, cbu_ref)
    acc_ref[...] += _dot(act.astype(BF16), wd_ref[...])

    @pl.when(j == pl.num_programs(1) - 1)
    def _():
        o_ref[...] = x_ref[...] + acc_ref[...]


def _conv_ffn(x2, g, w_up, conv_w, conv_b, w_down, s):
    n = x2.shape[0]
    tm = 512
    ft = FFN_FT
    nf = FFN_DIM // ft
    halo = SUBLANES
    hb = tm // halo
    w_up = w_up.astype(BF16)
    return pl.pallas_call(
        functools.partial(_ffn_body, tiles_per_seq=s // tm),
        name="convffn",
        grid=(n // tm, nf),
        in_specs=[pl.BlockSpec((tm, D_MODEL), lambda i, j: (i, 0)),
                  pl.BlockSpec((halo, D_MODEL), lambda i, j: (jnp.maximum(i * hb - 1, 0), 0)),
                  pl.BlockSpec((1, D_MODEL), lambda i, j: (0, 0)),
                  pl.BlockSpec((D_MODEL, ft), lambda i, j: (0, j)),
                  pl.BlockSpec((D_MODEL, ft), lambda i, j: (0, nf + j)),
                  pl.BlockSpec((FFN_CONV, ft), lambda i, j: (0, j)),
                  pl.BlockSpec((FFN_CONV, ft), lambda i, j: (0, nf + j)),
                  pl.BlockSpec((1, ft), lambda i, j: (0, j)),
                  pl.BlockSpec((1, ft), lambda i, j: (0, nf + j)),
                  pl.BlockSpec((ft, D_MODEL), lambda i, j: (j, 0))],
        out_specs=pl.BlockSpec((tm, D_MODEL), lambda i, j: (i, 0)),
        out_shape=jax.ShapeDtypeStruct((n, D_MODEL), F32),
        scratch_shapes=[pltpu.VMEM((tm + halo, D_MODEL), BF16),
                        pltpu.VMEM((tm + halo, ft), F32),
                        pltpu.VMEM((tm + halo, ft), F32),
                        pltpu.VMEM((tm, D_MODEL), F32)],
        compiler_params=_cparams(("parallel", "arbitrary"), 56),
    )(x2, x2, g, w_up, w_up, conv_w, conv_w, conv_b[None, :], conv_b[None, :], w_down.astype(BF16))


def _rope_angles(positions):
    half = ROPE_DIM // 2
    inv_freq = ROPE_THETA ** (-jnp.arange(0, ROPE_DIM, 2, dtype=F32) / ROPE_DIM)
    lane = np.arange(LANES) % NSA_HEAD_DIM
    freq = jnp.where(jnp.asarray(lane < ROPE_DIM), inv_freq[jnp.asarray(lane % half)], 0.0)
    return positions.astype(F32).reshape(-1, 1) * freq[None, :]


def kernel(x, positions, attn_norm_g, ffn_norm_g, w_in, nsa_q_norm_g, nsa_k_norm_g, nsa_cmp_pos_k, nsa_cmp_pos_v, nsa_cmp_k_w1, nsa_cmp_k_w2, nsa_cmp_v_w1, nsa_cmp_v_w2, hgrn_lb_logits, hgrn_norm_g, m2_conv_w, m2_conv_b, m2_dt_bias, m2_a_log, m2_d_skip, m2_norm_g, w_branch_nsa, w_branch_hgrn, w_branch_m2, w_out, ffn_w_up, ffn_conv_w, ffn_conv_b, ffn_w_down):
    b, s, _ = x.shape
    depth = w_in.shape[0]
    x2 = x.reshape(b * s, D_MODEL)
    ang = _rope_angles(positions)
    for l in range(depth):
        p = _inproj(x2, attn_norm_g[l][None, :], _pack_w_in(w_in[l]))
        qn, qr, ks, vs, kw, vw, gs = _nsaprep(p, ang, nsa_q_norm_g[l], nsa_k_norm_g[l], b, s)
        kc, vc = _compress(p, nsa_cmp_pos_k[l], nsa_cmp_pos_v[l], nsa_cmp_k_w1[l], nsa_cmp_k_w2[l],
                           nsa_cmp_v_w1[l], nsa_cmp_v_w2[l], nsa_k_norm_g[l, 0], b, s)
        ya = _nsa_attention(qn, qr, kc, vc, ks, vs, kw, vw, gs, b, s)
        yb = _hgrn(p, hgrn_lb_logits, hgrn_norm_g[l], l, b, s)
        yc = _mamba2(p, m2_conv_w[l], m2_conv_b[l], m2_dt_bias[l], m2_a_log[l], m2_d_skip[l], m2_norm_g[l], b, s)
        x2 = _merge(x2, ya, yb, yc, p, w_branch_nsa[l], w_branch_hgrn[l], w_branch_m2[l], w_out[l])
        x2 = _conv_ffn(x2, ffn_norm_g[l][None, :], ffn_w_up[l], ffn_conv_w[l], ffn_conv_b[l], ffn_w_down[l], s)
    return x2.reshape(b, s, D_MODEL)
```

```python
import functools
import math

import numpy as np
import jax
import jax.numpy as jnp
from jax import lax
from jax.experimental import pallas as pl
from jax.experimental.pallas import tpu as pltpu

F32 = jnp.float32
BF16 = jnp.bfloat16

D_MODEL = 1024
NSA_HEADS = 8
NSA_KV_HEADS = 2
NSA_REP = NSA_HEADS // NSA_KV_HEADS
NSA_HEAD_DIM = 64
NSA_CMP_BLOCK = 32
NSA_CMP_STRIDE = 16
NSA_SEL_BLOCK = 64
NSA_TOP_N = 16
NSA_WINDOW = 512
NSA_CMP_HIDDEN = 256
NSA_Q_BLOCK = 128
NSA_BIG = 1e9
ROPE_THETA = 500000.0
ROPE_DIM = NSA_HEAD_DIM // 4
NSA_WIDTH = NSA_HEADS * NSA_HEAD_DIM
NSA_KV_WIDTH = NSA_KV_HEADS * NSA_HEAD_DIM
HG_HEADS = 4
HG_HEAD_DIM = 128
HG_WIDTH = HG_HEADS * HG_HEAD_DIM
HG_SUB = 16
M2_HEADS = 16
M2_HEAD_DIM = 64
M2_INNER = M2_HEADS * M2_HEAD_DIM
M2_GROUPS = 2
M2_STATE = 128
M2_CONV = 4
M2_CHUNK = 128
M2_BC = M2_GROUPS * M2_STATE
FFN_DIM = 2816
FFN_CONV = 3
EPS = 1e-6

NEG = -1e30
LANES = 128
SUBLANES = 8

C_Z = 0
C_XS = 1024
C_GA = 2048
C_GB = 3072
C_GC = 4096
C_Q = 5120
C_HQ = 5632
C_HF = 6144
C_HI = 6656
C_HG = 7168
C_B = 7680
C_C = 7936
C_KV = 8192
C_SMALL = 8960
P_DIM = 9216
SMALL_DT = 3 * NSA_HEADS

_SRC = np.cumsum([0, NSA_WIDTH, 6 * NSA_KV_WIDTH, 3 * NSA_HEADS, HG_WIDTH, HG_WIDTH, HG_WIDTH, HG_WIDTH,
                  M2_INNER, M2_INNER + 2 * M2_BC, M2_HEADS, 3 * D_MODEL]).tolist()


def _cparams(sem, vmem_mib):
    return pltpu.CompilerParams(dimension_semantics=sem, vmem_limit_bytes=vmem_mib * 1024 * 1024)


def _sigmoid(x):
    return 1.0 / (1.0 + jnp.exp(-x))


def _silu(x):
    return x * _sigmoid(x)


def _dot(a, b):
    return jnp.dot(a, b, preferred_element_type=F32)


def _dot_nt(a, b):
    return lax.dot_general(a, b, (((1,), (1,)), ((), ())), preferred_element_type=F32)


def _dot_tn(a, b):
    return lax.dot_general(a, b, (((0,), (0,)), ((), ())), preferred_element_type=F32)


def _dot_hi(a, b):
    return jnp.dot(a, b, preferred_element_type=F32, precision=lax.Precision.HIGHEST)


def _pack_w_in(w):
    o = _SRC
    xbc = o[8]
    pieces = [
        w[:, o[7]:o[8]],
        w[:, xbc:xbc + M2_INNER],
        w[:, o[10]:o[10] + D_MODEL],
        w[:, o[10] + D_MODEL:o[10] + 2 * D_MODEL],
        w[:, o[10] + 2 * D_MODEL:o[11]],
        w[:, o[0]:o[1]],
        w[:, o[3]:o[4]], w[:, o[4]:o[5]], w[:, o[5]:o[6]], w[:, o[6]:o[7]],
        w[:, xbc + M2_INNER:xbc + M2_INNER + M2_BC],
        w[:, xbc + M2_INNER + M2_BC:o[9]],
        w[:, o[1]:o[2]],
        w[:, o[2]:o[3]],
        w[:, o[9]:o[10]],
    ]
    packed = jnp.concatenate(pieces, axis=1)
    packed = jnp.pad(packed, ((0, 0), (0, P_DIM - packed.shape[1])))
    return packed.astype(BF16)


def _inproj_body(x_ref, g_ref, w_ref, o_ref, h_ref):
    @pl.when(pl.program_id(1) == 0)
    def _():
        x = x_ref[...]
        ms = jnp.mean(x * x, axis=-1, keepdims=True)
        h_ref[...] = (x * lax.rsqrt(ms + EPS) * g_ref[...]).astype(BF16)

    o_ref[...] = _dot(h_ref[...], w_ref[...])


def _inproj(x2, g, w):
    n = x2.shape[0]
    tm = 1024 if n % 1024 == 0 else 512
    tn = 1024
    return pl.pallas_call(
        _inproj_body,
        name="inproj",
        grid=(n // tm, P_DIM // tn),
        in_specs=[pl.BlockSpec((tm, D_MODEL), lambda i, j: (i, 0)),
                  pl.BlockSpec((1, D_MODEL), lambda i, j: (0, 0)),
                  pl.BlockSpec((D_MODEL, tn), lambda i, j: (0, j))],
        out_specs=pl.BlockSpec((tm, tn), lambda i, j: (i, j)),
        out_shape=jax.ShapeDtypeStruct((n, P_DIM), F32),
        scratch_shapes=[pltpu.VMEM((tm, D_MODEL), BF16)],
        compiler_params=_cparams(("parallel", "arbitrary"), 48),
    )(x2, g, w)


def _head_sumsq(x, bd_ref):
    return _dot_hi(x * x, bd_ref[...])


def _rope(y, cos_t, sa_t, sb_t):
    w = y.shape[-1]
    return y * cos_t + pltpu.roll(y, w - ROPE_DIM // 2, 1) * sa_t + pltpu.roll(y, ROPE_DIM // 2, 1) * sb_t


def _nsaprep_body(q_ref, ksel_ref, vsel_ref, kwin_ref, vwin_ref, small_ref, ang_ref,
                  bdq_ref, bdk_ref, qg_ref, kg_ref,
                  qn_o, qr_o, ks_o, vs_o, kw_o, vw_o, gs_o):
    dh = NSA_HEAD_DIM
    half = ROPE_DIM // 2
    ang = ang_ref[...]
    lane = lax.broadcasted_iota(jnp.int32, ang.shape, 1) % dh
    cos_t = jnp.cos(ang)
    sin_t = jnp.sin(ang)
    sa_t = jnp.where(lane < half, -sin_t, 0.0)
    sb_t = jnp.where((lane >= half) & (lane < ROPE_DIM), sin_t, 0.0)
    rep = NSA_WIDTH // LANES
    cos_q = jnp.concatenate([cos_t] * rep, axis=1)
    sa_q = jnp.concatenate([sa_t] * rep, axis=1)
    sb_q = jnp.concatenate([sb_t] * rep, axis=1)

    scale = dh ** -0.5
    q = q_ref[...]
    qn = q * lax.rsqrt(_head_sumsq(q, bdq_ref) * (1.0 / dh) + EPS) * qg_ref[...]
    qr = _rope(qn, cos_q, sa_q, sb_q)
    qn_s = (qn * scale).astype(BF16)
    qr_s = (qr * scale).astype(BF16)
    zpad = jnp.zeros((q.shape[0], LANES - dh), BF16)
    for h in range(NSA_HEADS):
        qn_o[0, h] = qn_s[:, h * dh:(h + 1) * dh]
        qr_o[0, h] = jnp.concatenate([qr_s[:, h * dh:(h + 1) * dh], zpad], axis=1)

    def knorm(k_ref, row):
        k = k_ref[...]
        kn = k * lax.rsqrt(_head_sumsq(k, bdk_ref) * (1.0 / dh) + EPS) * kg_ref[row:row + 1, :]
        return _rope(kn, cos_t, sa_t, sb_t).astype(BF16)

    ks = knorm(ksel_ref, 1)
    kw = knorm(kwin_ref, 2)
    vs = vsel_ref[...].astype(BF16)
    vw = vwin_ref[...].astype(BF16)
    for g in range(NSA_KV_HEADS):
        ks_o[0, g] = jnp.concatenate([ks[:, g * dh:(g + 1) * dh], zpad], axis=1)
        kw_o[0, g] = jnp.concatenate([kw[:, g * dh:(g + 1) * dh], zpad], axis=1)
        vs_o[0, g] = vs[:, g * dh:(g + 1) * dh]
        vw_o[0, g] = vw[:, g * dh:(g + 1) * dh]

    sg = _sigmoid(small_ref[...])
    gs_o[0, 0] = sg
    gs_o[0, 1] = pltpu.roll(sg, LANES - 3 * NSA_REP, 1)


def _nsaprep(p, ang, qg, kg, b, s):
    t = 512
    nt = s // t
    dh = NSA_HEAD_DIM
    bdq = jnp.asarray(np.kron(np.eye(NSA_HEADS), np.ones((dh, dh))), F32)
    bdk = jnp.asarray(np.kron(np.eye(NSA_KV_HEADS), np.ones((dh, dh))), F32)
    qg_t = jnp.tile(qg, NSA_HEADS)[None, :]
    kg_t = jnp.tile(kg, (1, NSA_KV_HEADS))
    kvb = C_KV // LANES

    def col(width, cb):
        return pl.BlockSpec((t, width), lambda bi, i: (bi * nt + i, cb))

    def full(shape):
        return pl.BlockSpec(shape, lambda bi, i: (0,) * len(shape))

    hm = lambda heads, w=dh: pl.BlockSpec((1, heads, t, w), lambda bi, i: (bi, 0, i, 0))
    out_shape = (
        jax.ShapeDtypeStruct((b, NSA_HEADS, s, dh), BF16),
        jax.ShapeDtypeStruct((b, NSA_HEADS, s, LANES), BF16),
        jax.ShapeDtypeStruct((b, NSA_KV_HEADS, s, LANES), BF16),
        jax.ShapeDtypeStruct((b, NSA_KV_HEADS, s, dh), BF16),
        jax.ShapeDtypeStruct((b, NSA_KV_HEADS, s, LANES), BF16),
        jax.ShapeDtypeStruct((b, NSA_KV_HEADS, s, dh), BF16),
        jax.ShapeDtypeStruct((b, NSA_KV_HEADS, s, LANES), F32),
    )
    return pl.pallas_call(
        _nsaprep_body,
        name="nsaprep",
        grid=(b, nt),
        in_specs=[col(NSA_WIDTH, C_Q // NSA_WIDTH),
                  col(LANES, kvb + 2), col(LANES, kvb + 3), col(LANES, kvb + 4), col(LANES, kvb + 5),
                  col(LANES, C_SMALL // LANES),
                  col(LANES, 0),
                  full((NSA_WIDTH, NSA_WIDTH)), full((LANES, LANES)),
                  full((1, NSA_WIDTH)), full((3, LANES))],
        out_specs=(hm(NSA_HEADS), hm(NSA_HEADS, LANES), hm(NSA_KV_HEADS, LANES), hm(NSA_KV_HEADS),
                   hm(NSA_KV_HEADS, LANES), hm(NSA_KV_HEADS),
                   pl.BlockSpec((1, NSA_KV_HEADS, t, LANES), lambda bi, i: (bi, 0, i, 0))),
        out_shape=out_shape,
        compiler_params=_cparams(("parallel", "parallel"), 40),
    )(p, p, p, p, p, p, ang, bdq, bdk, qg_t, kg_t)


def _compress_body(rk_ref, rv_ref, pk_ref, pv_ref, kw1_ref, kw2_ref, vw1_ref, vw2_ref, kg_ref,
                   kc_o, vc_o):
    half = NSA_CMP_STRIDE * NSA_HEAD_DIM

    def mlp(r_ref, p_ref, w1_ref, w2_ref):
        r = r_ref[0, 0]
        nc = r.shape[0]
        top = _dot(r, w1_ref[0:half, :])
        bot = _dot(r, w1_ref[half:2 * half, :])
        posb = _dot(p_ref[...], w1_ref[...])[0:1, :]
        hid = top + pltpu.roll(bot, nc - 1, 0) + posb
        act = jax.nn.gelu(hid, approximate=True)
        return _dot(act.astype(BF16), w2_ref[...])

    kc = mlp(rk_ref, pk_ref, kw1_ref, kw2_ref)
    ms = jnp.mean(kc * kc, axis=-1, keepdims=True)
    kc_o[0, 0] = (kc * lax.rsqrt(ms + EPS) * kg_ref[...]).astype(BF16)
    vc_o[0, 0] = mlp(rv_ref, pv_ref, vw1_ref, vw2_ref).astype(BF16)


def _compress(p, pos_k, pos_v, kw1, kw2, vw1, vw2, kg0, b, s):
    dh = NSA_HEAD_DIM
    g = NSA_KV_HEADS
    nc = s // NSA_CMP_STRIDE
    width = NSA_CMP_STRIDE * dh
    raw = p[:, C_KV:C_KV + 2 * LANES].reshape(b, nc, NSA_CMP_STRIDE, 2, g, dh)
    raw = raw.transpose(3, 0, 4, 1, 2, 5).reshape(2, b, g, nc, width).astype(BF16)

    def pos_rows(pe):
        return jnp.broadcast_to(pe.reshape(1, NSA_CMP_BLOCK * dh), (SUBLANES, NSA_CMP_BLOCK * dh)).astype(BF16)

    blk = pl.BlockSpec((1, 1, nc, width), lambda bi, gi: (bi, gi, 0, 0))

    def full(shape):
        return pl.BlockSpec(shape, lambda bi, gi: (0,) * len(shape))

    oblk = pl.BlockSpec((1, 1, nc, dh), lambda bi, gi: (bi, gi, 0, 0))
    return pl.pallas_call(
        _compress_body,
        name="nsacompress",
        grid=(b, g),
        in_specs=[blk, blk,
                  full((SUBLANES, 2 * width)), full((SUBLANES, 2 * width)),
                  full((2 * width, NSA_CMP_HIDDEN)), full((NSA_CMP_HIDDEN, dh)),
                  full((2 * width, NSA_CMP_HIDDEN)), full((NSA_CMP_HIDDEN, dh)),
                  full((1, dh))],
        out_specs=(oblk, oblk),
        out_shape=(jax.ShapeDtypeStruct((b, g, nc, dh), BF16),
                   jax.ShapeDtypeStruct((b, g, nc, dh), BF16)),
        compiler_params=_cparams(("parallel", "parallel"), 40),
    )(raw[0], raw[1], pos_rows(pos_k), pos_rows(pos_v),
      kw1.astype(BF16), kw2.astype(BF16), vw1.astype(BF16), vw2.astype(BF16), kg0[None, :])


NSA_KT = 512


def _softmax_rows(s):
    m = jnp.max(s, axis=-1, keepdims=True)
    e = jnp.exp(s - m)
    l = jnp.sum(e, axis=-1, keepdims=True)
    return e, jnp.where(m > 0.5 * NEG, 1.0 / l, 0.0)


def _nsa_body(qn_ref, qr_ref, kc_ref, vc_ref, ks_ref, vs_ref, kw_ref, vw_ref, gs_ref, c2st_ref, et_ref,
              o_ref, m_ref, l_ref, acc_ref, *, top_n):
    tq = NSA_Q_BLOCK
    rep = NSA_REP
    dh = NSA_HEAD_DIM
    rows = rep * tq
    s0 = pl.program_id(2) * tq
    qn = qn_ref[0].reshape(rows, dh)
    qr = qr_ref[0].reshape(rows, LANES)
    t_q = s0 + lax.broadcasted_iota(jnp.int32, (tq, 1), 0)

    def heads(bias):
        return jnp.concatenate([bias] * rep, axis=0)

    nc = kc_ref.shape[2]
    cj = lax.broadcasted_iota(jnp.int32, (1, nc), 1)
    cbias = jnp.where((cj * NSA_CMP_STRIDE + (NSA_CMP_BLOCK - 1)) <= t_q, 0.0, NEG)
    e, inv = _softmax_rows(_dot_nt(qn, kc_ref[0, 0]) + heads(cbias))
    p_cmp = e * inv
    o_cmp = _dot(p_cmp.astype(BF16), vc_ref[0, 0])

    psum = p_cmp[0:tq]
    for r in range(1, rep):
        psum = psum + p_cmp[r * tq:(r + 1) * tq]
    imp = lax.dot_general(c2st_ref[...], psum, (((1,), (1,)), ((), ())),
                          preferred_element_type=F32, precision=lax.Precision.HIGHEST)
    nsp = imp.shape[0]
    n_sel = ks_ref.shape[2] // NSA_SEL_BLOCK
    nb = lax.broadcasted_iota(jnp.int32, (nsp, 1), 0)
    t_row = s0 + lax.broadcasted_iota(jnp.int32, (1, tq), 1)
    cur = t_row // NSA_SEL_BLOCK
    forced = (nb == 0) | (nb == cur) | (nb == cur - 1)
    valid = nb * NSA_SEL_BLOCK <= t_row
    work = jnp.where(forced, NSA_BIG, jnp.where(valid, imp, -NSA_BIG))
    work = jnp.where(nb < n_sel, work, -jnp.inf)
    nbf = nb.astype(F32)
    sel_t = jnp.zeros((nsp, tq), F32)
    for _ in range(top_n):
        mx = jnp.max(work, axis=0, keepdims=True)
        idx = jnp.min(jnp.where(work == mx, nbf, float(nsp)), axis=0, keepdims=True)
        hit = nbf == idx
        sel_t = jnp.where(hit, 1.0, sel_t)
        work = jnp.where(hit, -jnp.inf, work)
    sel = sel_t.T

    wlen = NSA_WINDOW + tq
    kstart = pl.multiple_of(jnp.maximum(s0 - NSA_WINDOW, 0), tq)
    kpos = kstart + lax.broadcasted_iota(jnp.int32, (1, wlen), 1)
    wbias = jnp.where((kpos <= t_q) & ((t_q - kpos) < NSA_WINDOW), 0.0, NEG)
    ew, invw = _softmax_rows(_dot_nt(qr, kw_ref[0, 0, pl.ds(kstart, wlen), :]) + heads(wbias))
    o_win = _dot(ew.astype(BF16), vw_ref[0, 0, pl.ds(kstart, wlen), :]) * invw

    selbias = ((sel - 1.0) * -NEG).astype(BF16)
    qx = jnp.concatenate([heads(selbias), qr], axis=1)
    kt = NSA_KT
    m_ref[...] = jnp.full((rows, LANES), NEG, F32)
    l_ref[...] = jnp.zeros((rows, LANES), F32)
    acc_ref[...] = jnp.zeros((rows, dh), F32)

    def sel_tile(j, bias):
        k0 = pl.multiple_of(j * kt, kt)
        kx = jnp.concatenate([et_ref[pl.ds(k0, kt), :], ks_ref[0, 0, pl.ds(k0, kt), :]], axis=1)
        s = _dot_nt(qx, kx)
        if bias is not None:
            s = s + heads(bias)
        m_old = m_ref[...]
        m_new = jnp.maximum(m_old, jnp.max(s, axis=-1, keepdims=True))
        alpha = jnp.exp(m_old - m_new)
        pe = jnp.exp(s - jnp.concatenate([m_new] * (kt // LANES), axis=1))
        l_ref[...] = alpha * l_ref[...] + jnp.sum(pe, axis=-1, keepdims=True)
        acc_ref[...] = alpha[:, 0:dh] * acc_ref[...] + _dot(pe.astype(BF16), vs_ref[0, 0, pl.ds(k0, kt), :])
        m_ref[...] = m_new

    n_full = s0 // kt

    def full_tile(j, carry):
        sel_tile(j, None)
        return carry

    lax.fori_loop(0, n_full, full_tile, 0)
    dcr = lax.broadcasted_iota(jnp.int32, (tq, kt), 1) - lax.broadcasted_iota(jnp.int32, (tq, kt), 0)
    sel_tile(n_full, jnp.where(dcr <= s0 - n_full * kt, 0.0, NEG))
    o_sel = acc_ref[...] * (1.0 / l_ref[:, 0:dh])

    gs = gs_ref[0, 0]

    def gate(c):
        return jnp.concatenate([gs[:, 3 * r + c:3 * r + c + 1] for r in range(rep)], axis=0)

    o = gate(0) * o_cmp + gate(1) * o_sel + gate(2) * o_win
    o_ref[...] = jnp.concatenate([o[r * tq:(r + 1) * tq] for r in range(rep)], axis=1)


def _nsa_attention(qn, qr, kc, vc, ks, vs, kw, vw, gs, b, s):
    tq = NSA_Q_BLOCK
    dh = NSA_HEAD_DIM
    g = NSA_KV_HEADS
    nq = s // tq
    nc = s // NSA_CMP_STRIDE
    n_sel = s // NSA_SEL_BLOCK
    nsp = -(-n_sel // LANES) * LANES
    top_n = min(NSA_TOP_N, n_sel)
    c_start = np.arange(nc) * NSA_CMP_STRIDE
    s_start = np.arange(nsp) * NSA_SEL_BLOCK
    overlap = np.clip(np.minimum(c_start[:, None] + NSA_CMP_BLOCK, s_start[None, :] + NSA_SEL_BLOCK)
                      - np.maximum(c_start[:, None], s_start[None, :]), 0, None)
    c2st = jnp.asarray((overlap / NSA_CMP_BLOCK).T, F32)
    et = jnp.asarray((np.arange(s) // NSA_SEL_BLOCK)[:, None] == np.arange(nsp)[None, :], BF16)

    def qblk(w):
        return pl.BlockSpec((1, NSA_REP, tq, w), lambda bi, gi, i: (bi, gi, i, 0))

    def whole(n, w=dh):
        return pl.BlockSpec((1, 1, n, w), lambda bi, gi, i: (bi, gi, 0, 0))

    def full(shape):
        return pl.BlockSpec(shape, lambda bi, gi, i: (0,) * len(shape))

    return pl.pallas_call(
        functools.partial(_nsa_body, top_n=top_n),
        name="nsaattn",
        grid=(b, g, nq),
        in_specs=[qblk(dh), qblk(LANES), whole(nc), whole(nc), whole(s, LANES), whole(s), whole(s, LANES), whole(s),
                  pl.BlockSpec((1, 1, tq, LANES), lambda bi, gi, i: (bi, gi, i, 0)),
                  full((nsp, nc)), full((s, nsp))],
        out_specs=pl.BlockSpec((tq, NSA_REP * dh), lambda bi, gi, i: (bi * nq + i, gi)),
        out_shape=jax.ShapeDtypeStruct((b * s, NSA_WIDTH), F32),
        scratch_shapes=[pltpu.VMEM((NSA_REP * tq, LANES), F32), pltpu.VMEM((NSA_REP * tq, LANES), F32),
                        pltpu.VMEM((NSA_REP * tq, dh), F32)],
        compiler_params=_cparams(("parallel", "parallel", "arbitrary"), 56),
    )(qn, qr, kc, vc, ks, vs, kw, vw, gs, c2st, et)


HG_TILE = 256


def _hgrn_body(q_ref, f_ref, i_ref, g_ref, lbl_ref, ng_ref, tri_ref, o_ref,
               st_ref, gc_ref, k_ref, oacc_ref, *, layer):
    sub = HG_SUB
    dk = HG_HEAD_DIM
    th = q_ref.shape[0]

    @pl.when(pl.program_id(1) == 0)
    def _():
        st_ref[...] = jnp.zeros_like(st_ref)

    lbl = lbl_ref[...]
    el = jnp.exp(lbl - jnp.max(lbl, axis=0, keepdims=True))
    soft = el / jnp.sum(el, axis=0, keepdims=True)
    lb = jnp.zeros_like(soft[0:1])
    for d in range(1, layer + 1):
        lb = lb + soft[d:d + 1]

    f = lb + (1.0 - lb) * _sigmoid(f_ref[...])
    lf = jnp.log(f)
    k_ref[...] = 1.0 - f
    blk = tri_ref.shape[0]
    for c in range(th // blk):
        gc_ref[c * blk:(c + 1) * blk, :] = _dot_hi(tri_ref[...], lf[c * blk:(c + 1) * blk, :])

    srow = lax.broadcasted_iota(jnp.int32, (sub, 1), 0)

    def step(c, carry):
        r0 = pl.multiple_of(c * sub, sub)
        gall = gc_ref[pl.ds(r0, sub), :]
        qall = q_ref[pl.ds(r0, sub), :]
        kall = k_ref[pl.ds(r0, sub), :]
        vall = i_ref[pl.ds(r0, sub), :]
        outs = []
        for h in range(HG_HEADS):
            cs = slice(h * dk, (h + 1) * dk)
            g, q, kk, v = gall[:, cs], qall[:, cs], kall[:, cs], vall[:, cs]
            st = st_ref[h]
            o = _dot_nt((q * jnp.exp(g)).astype(BF16), st.astype(BF16))
            for t in range(sub):
                d = jnp.where(srow <= t, g[t:t + 1, :] - g, NEG)
                w = (q[t:t + 1, :] * kk) * jnp.exp(d)
                r = jnp.sum(w, axis=-1, keepdims=True)
                ot = jnp.sum(r * v, axis=0, keepdims=True)
                o = o + jnp.where(srow == t, ot, 0.0)
            g_last = g[sub - 1:sub, :]
            kt = kk * jnp.exp(g_last - g)
            st_ref[h] = st * jnp.exp(g_last) + _dot_tn(v.astype(BF16), kt.astype(BF16))
            outs.append(o)
        oacc_ref[pl.ds(r0, sub), :] = jnp.concatenate(outs, axis=1)
        return carry

    lax.fori_loop(0, th // sub, step, 0)

    o = oacc_ref[...]
    parts = []
    for h in range(HG_HEADS):
        oh = o[:, h * dk:(h + 1) * dk]
        ms = jnp.mean(oh * oh, axis=-1, keepdims=True)
        parts.append(oh * lax.rsqrt(ms + EPS) * ng_ref[...])
    o_ref[...] = jnp.concatenate(parts, axis=1) * _silu(g_ref[...])


def _hgrn(p, lb_logits, norm_g, layer, b, s):
    th = HG_TILE
    nt = s // th
    blk = 64
    tri = jnp.asarray(np.kron(np.eye(blk // HG_SUB), np.tril(np.ones((HG_SUB, HG_SUB)))), F32)
    depth = lb_logits.shape[0]

    def col(cb):
        return pl.BlockSpec((th, HG_WIDTH), lambda bi, i: (bi * nt + i, cb))

    def full(shape):
        return pl.BlockSpec(shape, lambda bi, i: (0,) * len(shape))

    return pl.pallas_call(
        functools.partial(_hgrn_body, layer=layer),
        name="hgrn",
        grid=(b, nt),
        in_specs=[col(C_HQ // HG_WIDTH), col(C_HF // HG_WIDTH), col(C_HI // HG_WIDTH), col(C_HG // HG_WIDTH),
                  full((depth, HG_WIDTH)), full((1, HG_HEAD_DIM)), full((blk, blk))],
        out_specs=pl.BlockSpec((th, HG_WIDTH), lambda bi, i: (bi * nt + i, 0)),
        out_shape=jax.ShapeDtypeStruct((b * s, HG_WIDTH), F32),
        scratch_shapes=[pltpu.VMEM((HG_HEADS, HG_HEAD_DIM, HG_HEAD_DIM), F32),
                        pltpu.VMEM((th, HG_WIDTH), F32),
                        pltpu.VMEM((th, HG_WIDTH), F32),
                        pltpu.VMEM((th, HG_WIDTH), F32)],
        compiler_params=_cparams(("parallel", "arbitrary"), 40),
    )(p, p, p, p, lb_logits, norm_g[None, :], tri)


def _m2_body(z_ref, xs_ref, b_ref, c_ref, small_ref, cwx_ref, cwbc_ref, cbx_ref, cbbc_ref,
             dtb_ref, alog_ref, dskip_ref, ng_ref, tri_ref, trit_ref, e16_ref, o_ref,
             st_ref, px_ref, pbc_ref):
    ch = M2_CHUNK
    hp = M2_HEAD_DIM
    ns = M2_STATE
    halo = SUBLANES

    @pl.when(pl.program_id(1) == 0)
    def _():
        st_ref[...] = jnp.zeros_like(st_ref)
        px_ref[0:halo, :] = jnp.zeros((halo, M2_INNER), F32)
        pbc_ref[0:halo, :] = jnp.zeros((halo, 2 * M2_BC), F32)

    px_ref[halo:halo + ch, :] = xs_ref[...]
    pbc_ref[halo:halo + ch, 0:M2_BC] = b_ref[...]
    pbc_ref[halo:halo + ch, M2_BC:2 * M2_BC] = c_ref[...]

    def conv(p_ref, w_ref, bias_ref):
        acc = bias_ref[...]
        for k in range(M2_CONV):
            off = halo - (M2_CONV - 1) + k
            acc = acc + w_ref[k:k + 1, :] * p_ref[off:off + ch, :]
        return _silu(acc)

    xs = conv(px_ref, cwx_ref, cbx_ref)
    bc = conv(pbc_ref, cwbc_ref, cbbc_ref)
    px_ref[0:halo, :] = px_ref[ch:ch + halo, :]
    pbc_ref[0:halo, :] = pbc_ref[ch:ch + halo, :]

    dtr = small_ref[...] + dtb_ref[...]
    dt = jnp.maximum(dtr, 0.0) + jnp.log(1.0 + jnp.exp(-jnp.abs(dtr)))
    a = dt * (-jnp.exp(alog_ref[...]))
    a_cs = _dot_hi(tri_ref[...], a)
    a_cs_t = lax.dot_general(a, trit_ref[...], (((0,), (0,)), ((), ())),
                             preferred_element_type=F32, precision=lax.Precision.HIGHEST)
    a_last = a_cs[ch - 1:ch, :]
    e16 = e16_ref[...]
    dt_x = _dot_hi(dt, e16)
    dec_out_x = _dot_hi(jnp.exp(a_cs), e16)
    dec_st_x = _dot_hi(jnp.exp(a_last - a_cs), e16)
    xdt = xs * dt_x
    xdec = (xdt * dec_st_x).astype(BF16)
    xdt_b = xdt.astype(BF16)

    li = lax.broadcasted_iota(jnp.int32, (ch, ch), 0)
    si = lax.broadcasted_iota(jnp.int32, (ch, ch), 1)
    tril = li >= si
    ys = []
    hpg = M2_HEADS // M2_GROUPS
    for g in range(M2_GROUPS):
        bm = bc[:, g * ns:(g + 1) * ns].astype(BF16)
        cm = bc[:, M2_BC + g * ns:M2_BC + (g + 1) * ns].astype(BF16)
        cb = _dot_nt(cm, bm)
        for hh in range(hpg):
            h = g * hpg + hh
            hs = slice(h * hp, (h + 1) * hp)
            hl = SMALL_DT + h
            seg = jnp.where(tril, a_cs[:, hl:hl + 1] - a_cs_t[hl:hl + 1, :], NEG)
            y = _dot((cb * jnp.exp(seg)).astype(BF16), xdt_b[:, hs])
            st = st_ref[h]
            y = y + _dot_nt(cm, st.astype(BF16)) * dec_out_x[:, hs]
            st_ref[h] = st * jnp.exp(a_last[:, hl:hl + 1]) + _dot_tn(xdec[:, hs], bm)
            ys.append(y)
    y = jnp.concatenate(ys, axis=1) + dskip_ref[...] * xs
    y = y * _silu(z_ref[...])
    gw = M2_INNER // M2_GROUPS
    parts = []
    for g in range(M2_GROUPS):
        yg = y[:, g * gw:(g + 1) * gw]
        ms = jnp.mean(yg * yg, axis=-1, keepdims=True)
        parts.append(yg * lax.rsqrt(ms + EPS))
    o_ref[...] = jnp.concatenate(parts, axis=1) * ng_ref[...]


def _mamba2(p, conv_w, conv_b, dt_bias, a_log, d_skip, norm_g, b, s):
    ch = M2_CHUNK
    nt = s // ch
    tri = jnp.asarray(np.tril(np.ones((ch, ch))), F32)
    spread = np.zeros((LANES, M2_INNER))
    spread[SMALL_DT:SMALL_DT + M2_HEADS] = np.kron(np.eye(M2_HEADS), np.ones((1, M2_HEAD_DIM)))
    e16 = jnp.asarray(spread, F32)

    def lanes(v):
        return jnp.pad(v, (SMALL_DT, LANES - SMALL_DT - M2_HEADS))[None, :]

    def col(width, cb):
        return pl.BlockSpec((ch, width), lambda bi, i: (bi * nt + i, cb))

    def full(shape):
        return pl.BlockSpec(shape, lambda bi, i: (0,) * len(shape))

    return pl.pallas_call(
        _m2_body,
        name="mamba",
        grid=(b, nt),
        in_specs=[col(M2_INNER, C_Z // M2_INNER), col(M2_INNER, C_XS // M2_INNER),
                  col(M2_BC, C_B // M2_BC), col(M2_BC, C_C // M2_BC), col(LANES, C_SMALL // LANES),
                  full((M2_CONV, M2_INNER)), full((M2_CONV, 2 * M2_BC)),
                  full((1, M2_INNER)), full((1, 2 * M2_BC)),
                  full((1, LANES)), full((1, LANES)), full((1, M2_INNER)), full((1, M2_INNER)),
                  full((ch, ch)), full((ch, ch)), full((LANES, M2_INNER))],
        out_specs=pl.BlockSpec((ch, M2_INNER), lambda bi, i: (bi * nt + i, 0)),
        out_shape=jax.ShapeDtypeStruct((b * s, M2_INNER), F32),
        scratch_shapes=[pltpu.VMEM((M2_HEADS, M2_HEAD_DIM, M2_STATE), F32),
                        pltpu.VMEM((ch + SUBLANES, M2_INNER), F32),
                        pltpu.VMEM((ch + SUBLANES, 2 * M2_BC), F32)],
        compiler_params=_cparams(("parallel", "arbitrary"), 40),
    )(p, p, p, p, p,
      conv_w[:, :M2_INNER], conv_w[:, M2_INNER:], conv_b[None, :M2_INNER], conv_b[None, M2_INNER:],
      lanes(dt_bias), lanes(a_log), jnp.repeat(d_skip, M2_HEAD_DIM)[None, :], norm_g[None, :], tri, tri.T, e16)


def _merge_body(x_ref, ya_ref, yb_ref, yc_ref, ga_ref, gb_ref, gc_ref, wa_ref, wb_ref, wc_ref, wo_ref, o_ref):
    merged = (_sigmoid(ga_ref[...]) * _dot(ya_ref[...].astype(BF16), wa_ref[...])
              + _sigmoid(gb_ref[...]) * _dot(yb_ref[...].astype(BF16), wb_ref[...])
              + _sigmoid(gc_ref[...]) * _dot(yc_ref[...].astype(BF16), wc_ref[...]))
    o_ref[...] = x_ref[...] + _dot(merged.astype(BF16), wo_ref[...])


def _merge(x2, ya, yb, yc, p, wa, wb, wc, wo):
    n = x2.shape[0]
    tm = 256

    def rows(width, cb=0):
        return pl.BlockSpec((tm, width), lambda i: (i, cb))

    def full(shape):
        return pl.BlockSpec(shape, lambda i: (0,) * len(shape))

    return pl.pallas_call(
        _merge_body,
        name="merge",
        grid=(n // tm,),
        in_specs=[rows(D_MODEL), rows(NSA_WIDTH), rows(HG_WIDTH), rows(M2_INNER),
                  rows(D_MODEL, C_GA // D_MODEL), rows(D_MODEL, C_GB // D_MODEL), rows(D_MODEL, C_GC // D_MODEL),
                  full((NSA_WIDTH, D_MODEL)), full((HG_WIDTH, D_MODEL)), full((M2_INNER, D_MODEL)),
                  full((D_MODEL, D_MODEL))],
        out_specs=rows(D_MODEL),
        out_shape=jax.ShapeDtypeStruct((n, D_MODEL), F32),
        compiler_params=_cparams(("parallel",), 48),
    )(x2, ya, yb, yc, p, p, p, wa.astype(BF16), wb.astype(BF16), wc.astype(BF16), wo.astype(BF16))


FFN_FT = 1408


def _ffn_body(x_ref, xh_ref, g_ref, wg_ref, wu_ref, cwg_ref, cwu_ref, cbg_ref, cbu_ref, wd_ref, o_ref,
              h_ref, ug_ref, uu_ref, acc_ref, *, tiles_per_seq):
    halo = SUBLANES
    tm = x_ref.shape[0]
    j = pl.program_id(1)

    def norm(x):
        ms = jnp.mean(x * x, axis=-1, keepdims=True)
        return (x * lax.rsqrt(ms + EPS) * g_ref[...]).astype(BF16)

    @pl.when(j == 0)
    def _():
        first = (pl.program_id(0) % tiles_per_seq) == 0
        h_ref[0:halo, :] = jnp.where(first, jnp.zeros((halo, D_MODEL), BF16), norm(xh_ref[...]))
        h_ref[halo:halo + tm, :] = norm(x_ref[...])
        acc_ref[...] = jnp.zeros_like(acc_ref)

    h = h_ref[...]
    ug_ref[...] = _dot(h, wg_ref[...])
    uu_ref[...] = _dot(h, wu_ref[...])

    def conv(u_ref, w_ref, bias_ref):
        acc = bias_ref[...]
        for k in range(FFN_CONV):
            off = halo - (FFN_CONV - 1) + k
            acc = acc + w_ref[k:k + 1, :] * u_ref[off:off + tm, :]
        return acc

    act = _silu(conv(ug_ref, cwg_ref, cbg_ref)) * conv(uu_ref, cwu_ref, cbu_ref)
    acc_ref[...] += _dot(act.astype(BF16), wd_ref[...])

    @pl.when(j == pl.num_programs(1) - 1)
    def _():
        o_ref[...] = x_ref[...] + acc_ref[...]


def _conv_ffn(x2, g, w_up, conv_w, conv_b, w_down, s):
    n = x2.shape[0]
    tm = 512
    ft = FFN_FT
    nf = FFN_DIM // ft
    halo = SUBLANES
    hb = tm // halo
    w_up = w_up.astype(BF16)
    return pl.pallas_call(
        functools.partial(_ffn_body, tiles_per_seq=s // tm),
        name="convffn",
        grid=(n // tm, nf),
        in_specs=[pl.BlockSpec((tm, D_MODEL), lambda i, j: (i, 0)),
                  pl.BlockSpec((halo, D_MODEL), lambda i, j: (jnp.maximum(i * hb - 1, 0), 0)),
                  pl.BlockSpec((1, D_MODEL), lambda i, j: (0, 0)),
                  pl.BlockSpec((D_MODEL, ft), lambda i, j: (0, j)),
                  pl.BlockSpec((D_MODEL, ft), lambda i, j: (0, nf + j)),
                  pl.BlockSpec((FFN_CONV, ft), lambda i, j: (0, j)),
                  pl.BlockSpec((FFN_CONV, ft), lambda i, j: (0, nf + j)),
                  pl.BlockSpec((1, ft), lambda i, j: (0, j)),
                  pl.BlockSpec((1, ft), lambda i, j: (0, nf + j)),
                  pl.BlockSpec((ft, D_MODEL), lambda i, j: (j, 0))],
        out_specs=pl.BlockSpec((tm, D_MODEL), lambda i, j: (i, 0)),
        out_shape=jax.ShapeDtypeStruct((n, D_MODEL), F32),
        scratch_shapes=[pltpu.VMEM((tm + halo, D_MODEL), BF16),
                        pltpu.VMEM((tm + halo, ft), F32),
                        pltpu.VMEM((tm + halo, ft), F32),
                        pltpu.VMEM((tm, D_MODEL), F32)],
        compiler_params=_cparams(("parallel", "arbitrary"), 56),
    )(x2, x2, g, w_up, w_up, conv_w, conv_w, conv_b[None, :], conv_b[None, :], w_down.astype(BF16))


def _rope_angles(positions):
    half = ROPE_DIM // 2
    inv_freq = ROPE_THETA ** (-jnp.arange(0, ROPE_DIM, 2, dtype=F32) / ROPE_DIM)
    lane = np.arange(LANES) % NSA_HEAD_DIM
    freq = jnp.where(jnp.asarray(lane < ROPE_DIM), inv_freq[jnp.asarray(lane % half)], 0.0)
    return positions.astype(F32).reshape(-1, 1) * freq[None, :]


def kernel(x, positions, attn_norm_g, ffn_norm_g, w_in, nsa_q_norm_g, nsa_k_norm_g, nsa_cmp_pos_k, nsa_cmp_pos_v, nsa_cmp_k_w1, nsa_cmp_k_w2, nsa_cmp_v_w1, nsa_cmp_v_w2, hgrn_lb_logits, hgrn_norm_g, m2_conv_w, m2_conv_b, m2_dt_bias, m2_a_log, m2_d_skip, m2_norm_g, w_branch_nsa, w_branch_hgrn, w_branch_m2, w_out, ffn_w_up, ffn_conv_w, ffn_conv_b, ffn_w_down):
    b, s, _ = x.shape
    depth = w_in.shape[0]
    x2 = x.reshape(b * s, D_MODEL)
    ang = _rope_angles(positions)
    for l in range(depth):
        p = _inproj(x2, attn_norm_g[l][None, :], _pack_w_in(w_in[l]))
        qn, qr, ks, vs, kw, vw, gs = _nsaprep(p, ang, nsa_q_norm_g[l], nsa_k_norm_g[l], b, s)
        kc, vc = _compress(p, nsa_cmp_pos_k[l], nsa_cmp_pos_v[l], nsa_cmp_k_w1[l], nsa_cmp_k_w2[l],
                           nsa_cmp_v_w1[l], nsa_cmp_v_w2[l], nsa_k_norm_g[l, 0], b, s)
        ya = _nsa_attention(qn, qr, kc, vc, ks, vs, kw, vw, gs, b, s)
        yb = _hgrn(p, hgrn_lb_logits, hgrn_norm_g[l], l, b, s)
        yc = _mamba2(p, m2_conv_w[l], m2_conv_b[l], m2_dt_bias[l], m2_a_log[l], m2_d_skip[l], m2_norm_g[l], b, s)
        x2 = _merge(x2, ya, yb, yc, p, w_branch_nsa[l], w_branch_hgrn[l], w_branch_m2[l], w_out[l])
        x2 = _conv_ffn(x2, ffn_norm_g[l][None, :], ffn_w_up[l], ffn_conv_w[l], ffn_conv_b[l], ffn_w_down[l], s)
    return x2.reshape(b, s, D_MODEL)
```

```python
import functools
import math

import numpy as np
import jax
import jax.numpy as jnp
from jax import lax
from jax.experimental import pallas as pl
from jax.experimental.pallas import tpu as pltpu

F32 = jnp.float32
BF16 = jnp.bfloat16

D_MODEL = 1024
NSA_HEADS = 8
NSA_KV_HEADS = 2
NSA_REP = NSA_HEADS // NSA_KV_HEADS
NSA_HEAD_DIM = 64
NSA_CMP_BLOCK = 32
NSA_CMP_STRIDE = 16
NSA_SEL_BLOCK = 64
NSA_TOP_N = 16
NSA_WINDOW = 512
NSA_CMP_HIDDEN = 256
NSA_Q_BLOCK = 256
NSA_BIG = 1e9
ROPE_THETA = 500000.0
ROPE_DIM = NSA_HEAD_DIM // 4
NSA_WIDTH = NSA_HEADS * NSA_HEAD_DIM
NSA_KV_WIDTH = NSA_KV_HEADS * NSA_HEAD_DIM
HG_HEADS = 4
HG_HEAD_DIM = 128
HG_WIDTH = HG_HEADS * HG_HEAD_DIM
HG_SUB = 16
M2_HEADS = 16
M2_HEAD_DIM = 64
M2_INNER = M2_HEADS * M2_HEAD_DIM
M2_GROUPS = 2
M2_STATE = 128
M2_CONV = 4
M2_CHUNK = 128
M2_BC = M2_GROUPS * M2_STATE
FFN_DIM = 2816
FFN_CONV = 3
EPS = 1e-6

NEG = -1e30
NSA_Q_SCALE = NSA_HEAD_DIM ** -0.5 * math.log2(math.e)
LANES = 128
SUBLANES = 8

C_Z = 0
C_XS = 1024
C_GA = 2048
C_GB = 3072
C_GC = 4096
C_Q = 5120
C_HQ = 5632
C_HF = 6144
C_HI = 6656
C_HG = 7168
C_B = 7680
C_C = 7936
C_KV = 8192
C_SMALL = 8960
P_DIM = 9216
SMALL_DT = 3 * NSA_HEADS

_SRC = np.cumsum([0, NSA_WIDTH, 6 * NSA_KV_WIDTH, 3 * NSA_HEADS, HG_WIDTH, HG_WIDTH, HG_WIDTH, HG_WIDTH,
                  M2_INNER, M2_INNER + 2 * M2_BC, M2_HEADS, 3 * D_MODEL]).tolist()


def _cparams(sem, vmem_mib):
    return pltpu.CompilerParams(dimension_semantics=sem, vmem_limit_bytes=vmem_mib * 1024 * 1024)


def _sigmoid(x):
    return 1.0 / (1.0 + jnp.exp(-x))


def _silu(x):
    return x * _sigmoid(x)


def _dot(a, b):
    return jnp.dot(a, b, preferred_element_type=F32)


def _dot_nt(a, b):
    return lax.dot_general(a, b, (((1,), (1,)), ((), ())), preferred_element_type=F32)


def _dot_tn(a, b):
    return lax.dot_general(a, b, (((0,), (0,)), ((), ())), preferred_element_type=F32)


def _split3(a):
    hi = a.astype(BF16)
    r1 = a - hi.astype(F32)
    mid = r1.astype(BF16)
    lo = (r1 - mid.astype(F32)).astype(BF16)
    return hi, mid, lo


def _dot_split3(a, sel):
    hi, mid, lo = _split3(a)
    return _dot(hi, sel) + _dot(mid, sel) + _dot(lo, sel)


def _pack_w_in(w):
    o = _SRC
    xbc = o[8]
    pieces = [
        w[:, o[7]:o[8]],
        w[:, xbc:xbc + M2_INNER],
        w[:, o[10]:o[10] + D_MODEL],
        w[:, o[10] + D_MODEL:o[10] + 2 * D_MODEL],
        w[:, o[10] + 2 * D_MODEL:o[11]],
        w[:, o[0]:o[1]],
        w[:, o[3]:o[4]], w[:, o[4]:o[5]], w[:, o[5]:o[6]], w[:, o[6]:o[7]],
        w[:, xbc + M2_INNER:xbc + M2_INNER + M2_BC],
        w[:, xbc + M2_INNER + M2_BC:o[9]],
        w[:, o[1]:o[2]],
        w[:, o[2]:o[3]],
        w[:, o[9]:o[10]],
    ]
    packed = jnp.concatenate(pieces, axis=1)
    packed = jnp.pad(packed, ((0, 0), (0, P_DIM - packed.shape[1])))
    return packed.astype(BF16)


def _inproj_body(x_ref, g_ref, w_ref, o_ref, h_ref):
    @pl.when(pl.program_id(1) == 0)
    def _():
        x = x_ref[...]
        ms = jnp.mean(x * x, axis=-1, keepdims=True)
        h_ref[...] = (x * lax.rsqrt(ms + EPS) * g_ref[...]).astype(BF16)

    o_ref[...] = _dot(h_ref[...], w_ref[...])


def _inproj(x2, g, w):
    n = x2.shape[0]
    tm = 1024 if n % 1024 == 0 else 512
    tn = 1024
    return pl.pallas_call(
        _inproj_body,
        name="inproj",
        grid=(n // tm, P_DIM // tn),
        in_specs=[pl.BlockSpec((tm, D_MODEL), lambda i, j: (i, 0)),
                  pl.BlockSpec((1, D_MODEL), lambda i, j: (0, 0)),
                  pl.BlockSpec((D_MODEL, tn), lambda i, j: (0, j))],
        out_specs=pl.BlockSpec((tm, tn), lambda i, j: (i, j)),
        out_shape=jax.ShapeDtypeStruct((n, P_DIM), F32),
        scratch_shapes=[pltpu.VMEM((tm, D_MODEL), BF16)],
        compiler_params=_cparams(("parallel", "arbitrary"), 48),
    )(x2, g, w)


def _head_sumsq(x, bd_ref):
    return _dot_split3(x * x, bd_ref[...])


def _rope(y, cos_t, sa_t, sb_t):
    w = y.shape[-1]
    return y * cos_t + pltpu.roll(y, w - ROPE_DIM // 2, 1) * sa_t + pltpu.roll(y, ROPE_DIM // 2, 1) * sb_t


def _nsaprep_body(q_ref, ksel_ref, vsel_ref, kwin_ref, vwin_ref, small_ref, ang_ref,
                  bdq_ref, bdk_ref, qg_ref, kg_ref,
                  qn_o, qr_o, ks_o, vs_o, kw_o, vw_o, gs_o):
    dh = NSA_HEAD_DIM
    half = ROPE_DIM // 2
    ang = ang_ref[...]
    lane = lax.broadcasted_iota(jnp.int32, ang.shape, 1) % dh
    cos_t = jnp.cos(ang)
    sin_t = jnp.sin(ang)
    sa_t = jnp.where(lane < half, -sin_t, 0.0)
    sb_t = jnp.where((lane >= half) & (lane < ROPE_DIM), sin_t, 0.0)
    rep = NSA_WIDTH // LANES
    cos_q = jnp.concatenate([cos_t] * rep, axis=1)
    sa_q = jnp.concatenate([sa_t] * rep, axis=1)
    sb_q = jnp.concatenate([sb_t] * rep, axis=1)

    scale = NSA_Q_SCALE
    q = q_ref[...]
    qn = q * lax.rsqrt(_head_sumsq(q, bdq_ref) * (1.0 / dh) + EPS) * qg_ref[...]
    qr = _rope(qn, cos_q, sa_q, sb_q)
    qn_s = (qn * scale).astype(BF16)
    qr_s = (qr * scale).astype(BF16)
    zpad = jnp.zeros((q.shape[0], LANES - dh), BF16)
    for h in range(NSA_HEADS):
        qn_o[0, h] = qn_s[:, h * dh:(h + 1) * dh]
        qr_o[0, h] = jnp.concatenate([qr_s[:, h * dh:(h + 1) * dh], zpad], axis=1)

    def knorm(k_ref, row):
        k = k_ref[...]
        kn = k * lax.rsqrt(_head_sumsq(k, bdk_ref) * (1.0 / dh) + EPS) * kg_ref[row:row + 1, :]
        return _rope(kn, cos_t, sa_t, sb_t).astype(BF16)

    ks = knorm(ksel_ref, 1)
    kw = knorm(kwin_ref, 2)
    vs = vsel_ref[...].astype(BF16)
    vw = vwin_ref[...].astype(BF16)
    for g in range(NSA_KV_HEADS):
        ks_o[0, g] = jnp.concatenate([ks[:, g * dh:(g + 1) * dh], zpad], axis=1)
        kw_o[0, g] = jnp.concatenate([kw[:, g * dh:(g + 1) * dh], zpad], axis=1)
        vs_o[0, g] = vs[:, g * dh:(g + 1) * dh]
        vw_o[0, g] = vw[:, g * dh:(g + 1) * dh]

    sg = _sigmoid(small_ref[...])
    gs_o[0, 0] = sg
    gs_o[0, 1] = pltpu.roll(sg, LANES - 3 * NSA_REP, 1)


def _nsaprep(p, ang, qg, kg, b, s):
    t = 512
    nt = s // t
    dh = NSA_HEAD_DIM
    bdq = jnp.asarray(np.kron(np.eye(NSA_HEADS), np.ones((dh, dh))), BF16)
    bdk = jnp.asarray(np.kron(np.eye(NSA_KV_HEADS), np.ones((dh, dh))), BF16)
    qg_t = jnp.tile(qg, NSA_HEADS)[None, :]
    kg_t = jnp.tile(kg, (1, NSA_KV_HEADS))
    kvb = C_KV // LANES

    def col(width, cb):
        return pl.BlockSpec((t, width), lambda bi, i: (bi * nt + i, cb))

    def full(shape):
        return pl.BlockSpec(shape, lambda bi, i: (0,) * len(shape))

    hm = lambda heads, w=dh: pl.BlockSpec((1, heads, t, w), lambda bi, i: (bi, 0, i, 0))
    out_shape = (
        jax.ShapeDtypeStruct((b, NSA_HEADS, s, dh), BF16),
        jax.ShapeDtypeStruct((b, NSA_HEADS, s, LANES), BF16),
        jax.ShapeDtypeStruct((b, NSA_KV_HEADS, s, LANES), BF16),
        jax.ShapeDtypeStruct((b, NSA_KV_HEADS, s, dh), BF16),
        jax.ShapeDtypeStruct((b, NSA_KV_HEADS, s, LANES), BF16),
        jax.ShapeDtypeStruct((b, NSA_KV_HEADS, s, dh), BF16),
        jax.ShapeDtypeStruct((b, NSA_KV_HEADS, s, LANES), F32),
    )
    return pl.pallas_call(
        _nsaprep_body,
        name="nsaprep",
        grid=(b, nt),
        in_specs=[col(NSA_WIDTH, C_Q // NSA_WIDTH),
                  col(LANES, kvb + 2), col(LANES, kvb + 3), col(LANES, kvb + 4), col(LANES, kvb + 5),
                  col(LANES, C_SMALL // LANES),
                  col(LANES, 0),
                  full((NSA_WIDTH, NSA_WIDTH)), full((LANES, LANES)),
                  full((1, NSA_WIDTH)), full((3, LANES))],
        out_specs=(hm(NSA_HEADS), hm(NSA_HEADS, LANES), hm(NSA_KV_HEADS, LANES), hm(NSA_KV_HEADS),
                   hm(NSA_KV_HEADS, LANES), hm(NSA_KV_HEADS),
                   pl.BlockSpec((1, NSA_KV_HEADS, t, LANES), lambda bi, i: (bi, 0, i, 0))),
        out_shape=out_shape,
        compiler_params=_cparams(("parallel", "parallel"), 40),
    )(p, p, p, p, p, p, ang, bdq, bdk, qg_t, kg_t)


def _compress_body(rk_ref, rv_ref, pk_ref, pv_ref, kw1_ref, kw2_ref, vw1_ref, vw2_ref, kg_ref,
                   kc_o, vc_o):
    half = NSA_CMP_STRIDE * NSA_HEAD_DIM

    def mlp(r_ref, p_ref, w1_ref, w2_ref):
        r = r_ref[0, 0]
        nc = r.shape[0]
        top = _dot(r, w1_ref[0:half, :])
        bot = _dot(r, w1_ref[half:2 * half, :])
        posb = _dot(p_ref[...], w1_ref[...])[0:1, :]
        hid = top + pltpu.roll(bot, nc - 1, 0) + posb
        act = jax.nn.gelu(hid, approximate=True)
        return _dot(act.astype(BF16), w2_ref[...])

    kc = mlp(rk_ref, pk_ref, kw1_ref, kw2_ref)
    ms = jnp.mean(kc * kc, axis=-1, keepdims=True)
    kc_o[0, 0] = (kc * lax.rsqrt(ms + EPS) * kg_ref[...]).astype(BF16)
    vc_o[0, 0] = mlp(rv_ref, pv_ref, vw1_ref, vw2_ref).astype(BF16)


def _compress(p, pos_k, pos_v, kw1, kw2, vw1, vw2, kg0, b, s):
    dh = NSA_HEAD_DIM
    g = NSA_KV_HEADS
    nc = s // NSA_CMP_STRIDE
    width = NSA_CMP_STRIDE * dh
    raw = p[:, C_KV:C_KV + 2 * LANES].reshape(b, nc, NSA_CMP_STRIDE, 2, g, dh)
    raw = raw.transpose(3, 0, 4, 1, 2, 5).reshape(2, b, g, nc, width).astype(BF16)

    def pos_rows(pe):
        return jnp.broadcast_to(pe.reshape(1, NSA_CMP_BLOCK * dh), (SUBLANES, NSA_CMP_BLOCK * dh)).astype(BF16)

    blk = pl.BlockSpec((1, 1, nc, width), lambda bi, gi: (bi, gi, 0, 0))

    def full(shape):
        return pl.BlockSpec(shape, lambda bi, gi: (0,) * len(shape))

    oblk = pl.BlockSpec((1, 1, nc, dh), lambda bi, gi: (bi, gi, 0, 0))
    return pl.pallas_call(
        _compress_body,
        name="nsacompress",
        grid=(b, g),
        in_specs=[blk, blk,
                  full((SUBLANES, 2 * width)), full((SUBLANES, 2 * width)),
                  full((2 * width, NSA_CMP_HIDDEN)), full((NSA_CMP_HIDDEN, dh)),
                  full((2 * width, NSA_CMP_HIDDEN)), full((NSA_CMP_HIDDEN, dh)),
                  full((1, dh))],
        out_specs=(oblk, oblk),
        out_shape=(jax.ShapeDtypeStruct((b, g, nc, dh), BF16),
                   jax.ShapeDtypeStruct((b, g, nc, dh), BF16)),
        compiler_params=_cparams(("parallel", "parallel"), 40),
    )(raw[0], raw[1], pos_rows(pos_k), pos_rows(pos_v),
      kw1.astype(BF16), kw2.astype(BF16), vw1.astype(BF16), vw2.astype(BF16), kg0[None, :])


NSA_KT = 512


def _softmax_rows(s):
    m = jnp.max(s, axis=-1, keepdims=True)
    e = jnp.exp2(s - m)
    l = jnp.sum(e, axis=-1, keepdims=True)
    return e, jnp.where(m > 0.5 * NEG, 1.0 / l, 0.0)


def _nsa_body(qn_ref, qr_ref, kc_ref, vc_ref, ks_ref, vs_ref, kw_ref, vw_ref, gs_ref, c2st_ref, et_ref,
              o_ref, m_ref, l_ref, acc_ref, sa_ref, sb_ref, *, top_n):
    tq = NSA_Q_BLOCK
    rep = NSA_REP
    dh = NSA_HEAD_DIM
    rows = rep * tq
    s0 = pl.program_id(2) * tq
    qn = qn_ref[0].reshape(rows, dh)
    qr = qr_ref[0].reshape(rows, LANES)
    t_q = s0 + lax.broadcasted_iota(jnp.int32, (tq, 1), 0)

    def heads(bias):
        return jnp.concatenate([bias] * rep, axis=0)

    nc = kc_ref.shape[2]
    cj = lax.broadcasted_iota(jnp.int32, (1, nc), 1)
    cbias = jnp.where((cj * NSA_CMP_STRIDE + (NSA_CMP_BLOCK - 1)) <= t_q, 0.0, NEG)
    e, inv = _softmax_rows(_dot_nt(qn, kc_ref[0, 0]) + heads(cbias))
    p_cmp = e * inv
    o_cmp = _dot(p_cmp.astype(BF16), vc_ref[0, 0])

    psum = p_cmp[0:tq]
    for r in range(1, rep):
        psum = psum + p_cmp[r * tq:(r + 1) * tq]
    c2st = c2st_ref[...]
    imp = sum(_dot_nt(c2st, piece) for piece in _split3(psum))
    nsp = imp.shape[0]
    n_sel = ks_ref.shape[2] // NSA_SEL_BLOCK
    nb = lax.broadcasted_iota(jnp.int32, (nsp, 1), 0)
    t_row = s0 + lax.broadcasted_iota(jnp.int32, (1, tq), 1)
    cur = t_row // NSA_SEL_BLOCK
    forced = (nb == 0) | (nb == cur) | (nb == cur - 1)
    valid = nb * NSA_SEL_BLOCK <= t_row
    work = jnp.where(forced, NSA_BIG, jnp.where(valid, imp, -NSA_BIG))
    work = jnp.where(nb < n_sel, work, -jnp.inf)
    nbf = nb.astype(F32)
    sel_t = jnp.zeros((nsp, tq), F32)
    for _ in range(top_n):
        mx = jnp.max(work, axis=0, keepdims=True)
        idx = jnp.min(jnp.where(work == mx, nbf, float(nsp)), axis=0, keepdims=True)
        hit = nbf == idx
        sel_t = jnp.where(hit, 1.0, sel_t)
        work = jnp.where(hit, -jnp.inf, work)
    sel = sel_t.T

    wlen = NSA_WINDOW + tq
    kstart = pl.multiple_of(jnp.maximum(s0 - NSA_WINDOW, 0), tq)
    kpos = kstart + lax.broadcasted_iota(jnp.int32, (1, wlen), 1)
    wbias = jnp.where((kpos <= t_q) & ((t_q - kpos) < NSA_WINDOW), 0.0, NEG)
    ew, invw = _softmax_rows(_dot_nt(qr, kw_ref[0, 0, pl.ds(kstart, wlen), :]) + heads(wbias))
    o_win = _dot(ew.astype(BF16), vw_ref[0, 0, pl.ds(kstart, wlen), :]) * invw

    selbias = ((sel - 1.0) * -NEG).astype(BF16)
    qx = jnp.concatenate([heads(selbias), qr], axis=1)
    kt = NSA_KT
    m_ref[...] = jnp.full((rows, LANES), NEG, F32)
    l_ref[...] = jnp.zeros((rows, LANES), F32)
    acc_ref[...] = jnp.zeros((rows, dh), F32)

    def scores(j):
        k0 = pl.multiple_of(j * kt, kt)
        kx = jnp.concatenate([et_ref[pl.ds(k0, kt), :], ks_ref[0, 0, pl.ds(k0, kt), :]], axis=1)
        return _dot_nt(qx, kx)

    def update(s, j):
        k0 = pl.multiple_of(j * kt, kt)
        m_old = m_ref[...]
        m_new = jnp.maximum(m_old, jnp.max(s, axis=-1, keepdims=True))
        alpha = jnp.exp2(m_old - m_new)
        pe = jnp.exp2(s - jnp.concatenate([m_new] * (kt // LANES), axis=1))
        l_ref[...] = alpha * l_ref[...] + jnp.sum(pe, axis=-1, keepdims=True)
        acc_ref[...] = alpha[:, 0:dh] * acc_ref[...] + _dot(pe.astype(BF16), vs_ref[0, 0, pl.ds(k0, kt), :])
        m_ref[...] = m_new

    n_full = s0 // kt
    sa_ref[...] = scores(0)

    def tile_pair(jj, carry):
        j = 2 * jj
        sb_ref[...] = scores(j + 1)
        update(sa_ref[...], j)
        sa_ref[...] = scores(j + 2)
        update(sb_ref[...], j + 1)
        return carry

    lax.fori_loop(0, n_full // 2, tile_pair, 0)
    dcr = lax.broadcasted_iota(jnp.int32, (tq, kt), 1) - lax.broadcasted_iota(jnp.int32, (tq, kt), 0)
    dbias = heads(jnp.where(dcr <= s0 - n_full * kt, 0.0, NEG))

    @pl.when(n_full % 2 == 0)
    def _():
        update(sa_ref[...] + dbias, n_full)

    @pl.when(n_full % 2 == 1)
    def _():
        sb_ref[...] = scores(n_full)
        update(sa_ref[...], n_full - 1)
        update(sb_ref[...] + dbias, n_full)

    o_sel = acc_ref[...] * (1.0 / l_ref[:, 0:dh])

    gs = gs_ref[0, 0]

    def gate(c):
        return jnp.concatenate([gs[:, 3 * r + c:3 * r + c + 1] for r in range(rep)], axis=0)

    o = gate(0) * o_cmp + gate(1) * o_sel + gate(2) * o_win
    o_ref[...] = jnp.concatenate([o[r * tq:(r + 1) * tq] for r in range(rep)], axis=1)


def _nsa_attention(qn, qr, kc, vc, ks, vs, kw, vw, gs, b, s):
    tq = NSA_Q_BLOCK
    dh = NSA_HEAD_DIM
    g = NSA_KV_HEADS
    nq = s // tq
    nc = s // NSA_CMP_STRIDE
    n_sel = s // NSA_SEL_BLOCK
    nsp = -(-n_sel // LANES) * LANES
    top_n = min(NSA_TOP_N, n_sel)
    c_start = np.arange(nc) * NSA_CMP_STRIDE
    s_start = np.arange(nsp) * NSA_SEL_BLOCK
    overlap = np.clip(np.minimum(c_start[:, None] + NSA_CMP_BLOCK, s_start[None, :] + NSA_SEL_BLOCK)
                      - np.maximum(c_start[:, None], s_start[None, :]), 0, None)
    c2st = jnp.asarray((overlap / NSA_CMP_BLOCK).T, BF16)
    et = jnp.asarray((np.arange(s) // NSA_SEL_BLOCK)[:, None] == np.arange(nsp)[None, :], BF16)

    def qblk(w):
        return pl.BlockSpec((1, NSA_REP, tq, w), lambda bi, gi, i: (bi, gi, i, 0))

    def whole(n, w=dh):
        return pl.BlockSpec((1, 1, n, w), lambda bi, gi, i: (bi, gi, 0, 0))

    def full(shape):
        return pl.BlockSpec(shape, lambda bi, gi, i: (0,) * len(shape))

    return pl.pallas_call(
        functools.partial(_nsa_body, top_n=top_n),
        name="nsaattn",
        grid=(b, g, nq),
        in_specs=[qblk(dh), qblk(LANES), whole(nc), whole(nc), whole(s, LANES), whole(s), whole(s, LANES), whole(s),
                  pl.BlockSpec((1, 1, tq, LANES), lambda bi, gi, i: (bi, gi, i, 0)),
                  full((nsp, nc)), full((s, nsp))],
        out_specs=pl.BlockSpec((tq, NSA_REP * dh), lambda bi, gi, i: (bi * nq + i, gi)),
        out_shape=jax.ShapeDtypeStruct((b * s, NSA_WIDTH), F32),
        scratch_shapes=[pltpu.VMEM((NSA_REP * tq, LANES), F32), pltpu.VMEM((NSA_REP * tq, LANES), F32),
                        pltpu.VMEM((NSA_REP * tq, dh), F32),
                        pltpu.VMEM((NSA_REP * tq, NSA_KT), F32), pltpu.VMEM((NSA_REP * tq, NSA_KT), F32)],
        compiler_params=_cparams(("parallel", "parallel", "arbitrary"), 56),
    )(qn, qr, kc, vc, ks, vs, kw, vw, gs, c2st, et)


HG_TILE = 256


def _hgrn_body(q_ref, f_ref, i_ref, g_ref, lbl_ref, ng_ref, tri_ref, o_ref,
               st_ref, gc_ref, k_ref, oacc_ref, *, layer):
    sub = HG_SUB
    dk = HG_HEAD_DIM
    th = q_ref.shape[0]

    @pl.when(pl.program_id(1) == 0)
    def _():
        st_ref[...] = jnp.zeros_like(st_ref)

    lbl = lbl_ref[...]
    el = jnp.exp(lbl - jnp.max(lbl, axis=0, keepdims=True))
    soft = el / jnp.sum(el, axis=0, keepdims=True)
    lb = jnp.zeros_like(soft[0:1])
    for d in range(1, layer + 1):
        lb = lb + soft[d:d + 1]

    f = lb + (1.0 - lb) * _sigmoid(f_ref[...])
    lf = jnp.log(f) * math.log2(math.e)
    k_ref[...] = 1.0 - f
    blk = tri_ref.shape[0]
    for c in range(th // blk):
        gc_ref[c * blk:(c + 1) * blk, :] = sum(
            _dot(tri_ref[...], piece) for piece in _split3(lf[c * blk:(c + 1) * blk, :]))

    srow = lax.broadcasted_iota(jnp.int32, (sub, 1), 0)

    def step(c, carry):
        r0 = pl.multiple_of(c * sub, sub)
        gall = gc_ref[pl.ds(r0, sub), :]
        qall = q_ref[pl.ds(r0, sub), :]
        kall = k_ref[pl.ds(r0, sub), :]
        vall = i_ref[pl.ds(r0, sub), :]
        outs = []
        for h in range(HG_HEADS):
            cs = slice(h * dk, (h + 1) * dk)
            g, q, kk, v = gall[:, cs], qall[:, cs], kall[:, cs], vall[:, cs]
            st = st_ref[h]
            o = _dot_nt((q * jnp.exp2(g)).astype(BF16), st.astype(BF16))
            for t in range(sub):
                d = jnp.where(srow <= t, g[t:t + 1, :] - g, NEG)
                w = (q[t:t + 1, :] * kk) * jnp.exp2(d)
                r = jnp.sum(w, axis=-1, keepdims=True)
                ot = jnp.sum(r * v, axis=0, keepdims=True)
                o = o + jnp.where(srow == t, ot, 0.0)
            g_last = g[sub - 1:sub, :]
            kt = kk * jnp.exp2(g_last - g)
            st_ref[h] = st * jnp.exp2(g_last) + _dot_tn(v.astype(BF16), kt.astype(BF16))
            outs.append(o)
        oacc_ref[pl.ds(r0, sub), :] = jnp.concatenate(outs, axis=1)
        return carry

    lax.fori_loop(0, th // sub, step, 0)

    o = oacc_ref[...]
    parts = []
    for h in range(HG_HEADS):
        oh = o[:, h * dk:(h + 1) * dk]
        ms = jnp.mean(oh * oh, axis=-1, keepdims=True)
        parts.append(oh * lax.rsqrt(ms + EPS) * ng_ref[...])
    o_ref[...] = jnp.concatenate(parts, axis=1) * _silu(g_ref[...])


def _hgrn(p, lb_logits, norm_g, layer, b, s):
    th = HG_TILE
    nt = s // th
    blk = 64
    tri = jnp.asarray(np.kron(np.eye(blk // HG_SUB), np.tril(np.ones((HG_SUB, HG_SUB)))), BF16)
    depth = lb_logits.shape[0]

    def col(cb):
        return pl.BlockSpec((th, HG_WIDTH), lambda bi, i: (bi * nt + i, cb))

    def full(shape):
        return pl.BlockSpec(shape, lambda bi, i: (0,) * len(shape))

    return pl.pallas_call(
        functools.partial(_hgrn_body, layer=layer),
        name="hgrn",
        grid=(b, nt),
        in_specs=[col(C_HQ // HG_WIDTH), col(C_HF // HG_WIDTH), col(C_HI // HG_WIDTH), col(C_HG // HG_WIDTH),
                  full((depth, HG_WIDTH)), full((1, HG_HEAD_DIM)), full((blk, blk))],
        out_specs=pl.BlockSpec((th, HG_WIDTH), lambda bi, i: (bi * nt + i, 0)),
        out_shape=jax.ShapeDtypeStruct((b * s, HG_WIDTH), F32),
        scratch_shapes=[pltpu.VMEM((HG_HEADS, HG_HEAD_DIM, HG_HEAD_DIM), F32),
                        pltpu.VMEM((th, HG_WIDTH), F32),
                        pltpu.VMEM((th, HG_WIDTH), F32),
                        pltpu.VMEM((th, HG_WIDTH), F32)],
        compiler_params=_cparams(("parallel", "arbitrary"), 40),
    )(p, p, p, p, lb_logits, norm_g[None, :], tri)


def _m2_body(z_ref, xs_ref, b_ref, c_ref, small_ref, cwx_ref, cwbc_ref, cbx_ref, cbbc_ref,
             dtb_ref, alog_ref, dskip_ref, ng_ref, tri_ref, trit_ref, e16_ref, o_ref,
             st_ref, px_ref, pbc_ref):
    ch = M2_CHUNK
    hp = M2_HEAD_DIM
    ns = M2_STATE
    halo = SUBLANES

    @pl.when(pl.program_id(1) == 0)
    def _():
        st_ref[...] = jnp.zeros_like(st_ref)
        px_ref[0:halo, :] = jnp.zeros((halo, M2_INNER), F32)
        pbc_ref[0:halo, :] = jnp.zeros((halo, 2 * M2_BC), F32)

    px_ref[halo:halo + ch, :] = xs_ref[...]
    pbc_ref[halo:halo + ch, 0:M2_BC] = b_ref[...]
    pbc_ref[halo:halo + ch, M2_BC:2 * M2_BC] = c_ref[...]

    def conv(p_ref, w_ref, bias_ref):
        acc = bias_ref[...]
        for k in range(M2_CONV):
            off = halo - (M2_CONV - 1) + k
            acc = acc + w_ref[k:k + 1, :] * p_ref[off:off + ch, :]
        return _silu(acc)

    xs = conv(px_ref, cwx_ref, cbx_ref)
    bc = conv(pbc_ref, cwbc_ref, cbbc_ref)
    px_ref[0:halo, :] = px_ref[ch:ch + halo, :]
    pbc_ref[0:halo, :] = pbc_ref[ch:ch + halo, :]

    dtr = small_ref[...] + dtb_ref[...]
    dt = jnp.maximum(dtr, 0.0) + jnp.log(1.0 + jnp.exp(-jnp.abs(dtr)))
    a = dt * (-jnp.exp(alog_ref[...]))
    a3 = _split3(a)
    a_cs = sum(_dot(tri_ref[...], piece) for piece in a3)
    a_cs_t = sum(_dot_tn(piece, trit_ref[...]) for piece in a3)
    e16 = e16_ref[...]

    dt_x = _dot_split3(dt, e16)
    acs_x = _dot_split3(a_cs, e16)
    alast_x = acs_x[ch - 1:ch, :]
    xdt = xs * dt_x
    xdec = (xdt * jnp.exp(alast_x - acs_x)).astype(BF16)
    xdt_b = xdt.astype(BF16)
    dec_out_x = jnp.exp(acs_x)
    dec_chunk_x = jnp.exp(alast_x)

    li = lax.broadcasted_iota(jnp.int32, (ch, ch), 0)
    si = lax.broadcasted_iota(jnp.int32, (ch, ch), 1)
    tril = li >= si
    ys = []
    hpg = M2_HEADS // M2_GROUPS
    gw = hpg * hp
    for g in range(M2_GROUPS):
        gs_ = slice(g * gw, (g + 1) * gw)
        bm = bc[:, g * ns:(g + 1) * ns].astype(BF16)
        cm = bc[:, M2_BC + g * ns:M2_BC + (g + 1) * ns].astype(BF16)
        cb = _dot_nt(cm, bm)
        st = st_ref[g]
        ys.append(_dot(cm, st.astype(BF16)) * dec_out_x[:, gs_])
        st_ref[g] = st * dec_chunk_x[:, gs_] + _dot_tn(bm, xdec[:, gs_])
        for hh in range(hpg):
            h = g * hpg + hh
            hl = SMALL_DT + h
            seg = jnp.where(tril, a_cs[:, hl:hl + 1] - a_cs_t[hl:hl + 1, :], NEG)
            ys.append(_dot((cb * jnp.exp(seg)).astype(BF16), xdt_b[:, h * hp:(h + 1) * hp]))
    nd = 1 + hpg
    y_off = jnp.concatenate([ys[g * nd] for g in range(M2_GROUPS)], axis=1)
    y_diag = jnp.concatenate([ys[g * nd + 1 + hh] for g in range(M2_GROUPS) for hh in range(hpg)], axis=1)
    y = y_diag + y_off + dskip_ref[...] * xs
    y = y * _silu(z_ref[...])
    gw = M2_INNER // M2_GROUPS
    parts = []
    for g in range(M2_GROUPS):
        yg = y[:, g * gw:(g + 1) * gw]
        ms = jnp.mean(yg * yg, axis=-1, keepdims=True)
        parts.append(yg * lax.rsqrt(ms + EPS))
    o_ref[...] = jnp.concatenate(parts, axis=1) * ng_ref[...]


def _mamba2(p, conv_w, conv_b, dt_bias, a_log, d_skip, norm_g, b, s):
    ch = M2_CHUNK
    nt = s // ch
    tri = jnp.asarray(np.tril(np.ones((ch, ch))), BF16)
    spread = np.zeros((LANES, M2_INNER))
    spread[SMALL_DT:SMALL_DT + M2_HEADS] = np.kron(np.eye(M2_HEADS), np.ones((1, M2_HEAD_DIM)))
    e16 = jnp.asarray(spread, BF16)

    def lanes(v):
        return jnp.pad(v, (SMALL_DT, LANES - SMALL_DT - M2_HEADS))[None, :]

    def col(width, cb):
        return pl.BlockSpec((ch, width), lambda bi, i: (bi * nt + i, cb))

    def full(shape):
        return pl.BlockSpec(shape, lambda bi, i: (0,) * len(shape))

    return pl.pallas_call(
        _m2_body,
        name="mamba",
        grid=(b, nt),
        in_specs=[col(M2_INNER, C_Z // M2_INNER), col(M2_INNER, C_XS // M2_INNER),
                  col(M2_BC, C_B // M2_BC), col(M2_BC, C_C // M2_BC), col(LANES, C_SMALL // LANES),
                  full((M2_CONV, M2_INNER)), full((M2_CONV, 2 * M2_BC)),
                  full((1, M2_INNER)), full((1, 2 * M2_BC)),
                  full((1, LANES)), full((1, LANES)), full((1, M2_INNER)), full((1, M2_INNER)),
                  full((ch, ch)), full((ch, ch)), full((LANES, M2_INNER))],
        out_specs=pl.BlockSpec((ch, M2_INNER), lambda bi, i: (bi * nt + i, 0)),
        out_shape=jax.ShapeDtypeStruct((b * s, M2_INNER), F32),
        scratch_shapes=[pltpu.VMEM((M2_GROUPS, M2_STATE, M2_INNER // M2_GROUPS), F32),
                        pltpu.VMEM((ch + SUBLANES, M2_INNER), F32),
                        pltpu.VMEM((ch + SUBLANES, 2 * M2_BC), F32)],
        compiler_params=_cparams(("parallel", "arbitrary"), 40),
    )(p, p, p, p, p,
      conv_w[:, :M2_INNER], conv_w[:, M2_INNER:], conv_b[None, :M2_INNER], conv_b[None, M2_INNER:],
      lanes(dt_bias), lanes(a_log), jnp.repeat(d_skip, M2_HEAD_DIM)[None, :], norm_g[None, :], tri, tri.T, e16)


def _merge_body(x_ref, ya_ref, yb_ref, yc_ref, ga_ref, gb_ref, gc_ref, wa_ref, wb_ref, wc_ref, wo_ref, o_ref):
    merged = (_sigmoid(ga_ref[...]) * _dot(ya_ref[...].astype(BF16), wa_ref[...])
              + _sigmoid(gb_ref[...]) * _dot(yb_ref[...].astype(BF16), wb_ref[...])
              + _sigmoid(gc_ref[...]) * _dot(yc_ref[...].astype(BF16), wc_ref[...]))
    o_ref[...] = x_ref[...] + _dot(merged.astype(BF16), wo_ref[...])


def _merge(x2, ya, yb, yc, p, wa, wb, wc, wo):
    n = x2.shape[0]
    tm = 256

    def rows(width, cb=0):
        return pl.BlockSpec((tm, width), lambda i: (i, cb))

    def full(shape):
        return pl.BlockSpec(shape, lambda i: (0,) * len(shape))

    return pl.pallas_call(
        _merge_body,
        name="merge",
        grid=(n // tm,),
        in_specs=[rows(D_MODEL), rows(NSA_WIDTH), rows(HG_WIDTH), rows(M2_INNER),
                  rows(D_MODEL, C_GA // D_MODEL), rows(D_MODEL, C_GB // D_MODEL), rows(D_MODEL, C_GC // D_MODEL),
                  full((NSA_WIDTH, D_MODEL)), full((HG_WIDTH, D_MODEL)), full((M2_INNER, D_MODEL)),
                  full((D_MODEL, D_MODEL))],
        out_specs=rows(D_MODEL),
        out_shape=jax.ShapeDtypeStruct((n, D_MODEL), F32),
        compiler_params=_cparams(("parallel",), 48),
    )(x2, ya, yb, yc, p, p, p, wa.astype(BF16), wb.astype(BF16), wc.astype(BF16), wo.astype(BF16))


FFN_FT = 1408


def _ffn_body(x_ref, xh_ref, g_ref, wg_ref, wu_ref, cwg_ref, cwu_ref, cbg_ref, cbu_ref, wd_ref, o_ref,
              h_ref, ug_ref, uu_ref, acc_ref, *, tiles_per_seq):
    halo = SUBLANES
    tm = x_ref.shape[0]
    j = pl.program_id(1)

    def norm(x):
        ms = jnp.mean(x * x, axis=-1, keepdims=True)
        return (x * lax.rsqrt(ms + EPS) * g_ref[...]).astype(BF16)

    @pl.when(j == 0)
    def _():
        first = (pl.program_id(0) % tiles_per_seq) == 0
        h_ref[0:halo, :] = jnp.where(first, jnp.zeros((halo, D_MODEL), BF16), norm(xh_ref[...]))
        h_ref[halo:halo + tm, :] = norm(x_ref[...])
        acc_ref[...] = jnp.zeros_like(acc_ref)

    h = h_ref[...]
    ug_ref[...] = _dot(h, wg_ref[...])
    uu_ref[...] = _dot(h, wu_ref[...])

    def conv(u_ref, w_ref, bias_ref):
        acc = bias_ref[...]
        for k in range(FFN_CONV):
            off = halo - (FFN_CONV - 1) + k
            acc = acc + w_ref[k:k + 1, :] * u_ref[off:off + tm, :]
        return acc

    act = _silu(conv(ug_ref, cwg_ref, cbg_ref)) * conv(uu_ref, cwu_ref, cbu_ref)
    acc_ref[...] += _dot(act.astype(BF16), wd_ref[...])

    @pl.when(j == pl.num_programs(1) - 1)
    def _():
        o_ref[...] = x_ref[...] + acc_ref[...]


def _conv_ffn(x2, g, w_up, conv_w, conv_b, w_down, s):
    n = x2.shape[0]
    tm = 512
    ft = FFN_FT
    nf = FFN_DIM // ft
    halo = SUBLANES
    hb = tm // halo
    w_up = w_up.astype(BF16)
    return pl.pallas_call(
        functools.partial(_ffn_body, tiles_per_seq=s // tm),
        name="convffn",
        grid=(n // tm, nf),
        in_specs=[pl.BlockSpec((tm, D_MODEL), lambda i, j: (i, 0)),
                  pl.BlockSpec((halo, D_MODEL), lambda i, j: (jnp.maximum(i * hb - 1, 0), 0)),
                  pl.BlockSpec((1, D_MODEL), lambda i, j: (0, 0)),
                  pl.BlockSpec((D_MODEL, ft), lambda i, j: (0, j)),
                  pl.BlockSpec((D_MODEL, ft), lambda i, j: (0, nf + j)),
                  pl.BlockSpec((FFN_CONV, ft), lambda i, j: (0, j)),
                  pl.BlockSpec((FFN_CONV, ft), lambda i, j: (0, nf + j)),
                  pl.BlockSpec((1, ft), lambda i, j: (0, j)),
                  pl.BlockSpec((1, ft), lambda i, j: (0, nf + j)),
                  pl.BlockSpec((ft, D_MODEL), lambda i, j: (j, 0))],
        out_specs=pl.BlockSpec((tm, D_MODEL), lambda i, j: (i, 0)),
        out_shape=jax.ShapeDtypeStruct((n, D_MODEL), F32),
        scratch_shapes=[pltpu.VMEM((tm + halo, D_MODEL), BF16),
                        pltpu.VMEM((tm + halo, ft), F32),
                        pltpu.VMEM((tm + halo, ft), F32),
                        pltpu.VMEM((tm, D_MODEL), F32)],
        compiler_params=_cparams(("parallel", "arbitrary"), 56),
    )(x2, x2, g, w_up, w_up, conv_w, conv_w, conv_b[None, :], conv_b[None, :], w_down.astype(BF16))


def _rope_angles(positions):
    half = ROPE_DIM // 2
    inv_freq = ROPE_THETA ** (-jnp.arange(0, ROPE_DIM, 2, dtype=F32) / ROPE_DIM)
    lane = np.arange(LANES) % NSA_HEAD_DIM
    freq = jnp.where(jnp.asarray(lane < ROPE_DIM), inv_freq[jnp.asarray(lane % half)], 0.0)
    return positions.astype(F32).reshape(-1, 1) * freq[None, :]


def kernel(x, positions, attn_norm_g, ffn_norm_g, w_in, nsa_q_norm_g, nsa_k_norm_g, nsa_cmp_pos_k, nsa_cmp_pos_v, nsa_cmp_k_w1, nsa_cmp_k_w2, nsa_cmp_v_w1, nsa_cmp_v_w2, hgrn_lb_logits, hgrn_norm_g, m2_conv_w, m2_conv_b, m2_dt_bias, m2_a_log, m2_d_skip, m2_norm_g, w_branch_nsa, w_branch_hgrn, w_branch_m2, w_out, ffn_w_up, ffn_conv_w, ffn_conv_b, ffn_w_down):
    b, s, _ = x.shape
    depth = w_in.shape[0]
    x2 = x.reshape(b * s, D_MODEL)
    ang = _rope_angles(positions)
    for l in range(depth):
        p = _inproj(x2, attn_norm_g[l][None, :], _pack_w_in(w_in[l]))
        qn, qr, ks, vs, kw, vw, gs = _nsaprep(p, ang, nsa_q_norm_g[l], nsa_k_norm_g[l], b, s)
        kc, vc = _compress(p, nsa_cmp_pos_k[l], nsa_cmp_pos_v[l], nsa_cmp_k_w1[l], nsa_cmp_k_w2[l],
                           nsa_cmp_v_w1[l], nsa_cmp_v_w2[l], nsa_k_norm_g[l, 0], b, s)
        ya = _nsa_attention(qn, qr, kc, vc, ks, vs, kw, vw, gs, b, s)
        yb = _hgrn(p, hgrn_lb_logits, hgrn_norm_g[l], l, b, s)
        yc = _mamba2(p, m2_conv_w[l], m2_conv_b[l], m2_dt_bias[l], m2_a_log[l], m2_d_skip[l], m2_norm_g[l], b, s)
        x2 = _merge(x2, ya, yb, yc, p, w_branch_nsa[l], w_branch_hgrn[l], w_branch_m2[l], w_out[l])
        x2 = _conv_ffn(x2, ffn_norm_g[l][None, :], ffn_w_up[l], ffn_conv_w[l], ffn_conv_b[l], ffn_w_down[l], s)
    return x2.reshape(b, s, D_MODEL)
```

```python
import functools
import math

import numpy as np
import jax
import jax.numpy as jnp
from jax import lax
from jax.experimental import pallas as pl
from jax.experimental.pallas import tpu as pltpu

F32 = jnp.float32
BF16 = jnp.bfloat16

D_MODEL = 1024
NSA_HEADS = 8
NSA_KV_HEADS = 2
NSA_REP = NSA_HEADS // NSA_KV_HEADS
NSA_HEAD_DIM = 64
NSA_CMP_BLOCK = 32
NSA_CMP_STRIDE = 16
NSA_SEL_BLOCK = 64
NSA_TOP_N = 16
NSA_WINDOW = 512
NSA_CMP_HIDDEN = 256
NSA_Q_BLOCK = 256
NSA_BIG = 1e9
ROPE_THETA = 500000.0
ROPE_DIM = NSA_HEAD_DIM // 4
NSA_WIDTH = NSA_HEADS * NSA_HEAD_DIM
NSA_KV_WIDTH = NSA_KV_HEADS * NSA_HEAD_DIM
HG_HEADS = 4
HG_HEAD_DIM = 128
HG_WIDTH = HG_HEADS * HG_HEAD_DIM
HG_SUB = 16
M2_HEADS = 16
M2_HEAD_DIM = 64
M2_INNER = M2_HEADS * M2_HEAD_DIM
M2_GROUPS = 2
M2_STATE = 128
M2_CONV = 4
M2_CHUNK = 128
M2_BC = M2_GROUPS * M2_STATE
FFN_DIM = 2816
FFN_CONV = 3
EPS = 1e-6

NEG = -1e30
NSA_Q_SCALE = NSA_HEAD_DIM ** -0.5 * math.log2(math.e)
LANES = 128
SUBLANES = 8

C_Z = 0
C_XS = 1024
C_GA = 2048
C_GB = 3072
C_GC = 4096
C_Q = 5120
C_HQ = 5632
C_HF = 6144
C_HI = 6656
C_HG = 7168
C_B = 7680
C_C = 7936
C_KV = 8192
C_SMALL = 8960
P_DIM = 9216
SMALL_DT = 3 * NSA_HEADS

_SRC = np.cumsum([0, NSA_WIDTH, 6 * NSA_KV_WIDTH, 3 * NSA_HEADS, HG_WIDTH, HG_WIDTH, HG_WIDTH, HG_WIDTH,
                  M2_INNER, M2_INNER + 2 * M2_BC, M2_HEADS, 3 * D_MODEL]).tolist()


def _cparams(sem, vmem_mib):
    return pltpu.CompilerParams(dimension_semantics=sem, vmem_limit_bytes=vmem_mib * 1024 * 1024)


def _sigmoid(x):
    return 1.0 / (1.0 + jnp.exp(-x))


def _silu(x):
    return x * _sigmoid(x)


def _dot(a, b):
    return jnp.dot(a, b, preferred_element_type=F32)


def _dot_nt(a, b):
    return lax.dot_general(a, b, (((1,), (1,)), ((), ())), preferred_element_type=F32)


def _dot_tn(a, b):
    return lax.dot_general(a, b, (((0,), (0,)), ((), ())), preferred_element_type=F32)


def _split3(a):
    hi = a.astype(BF16)
    r1 = a - hi.astype(F32)
    mid = r1.astype(BF16)
    lo = (r1 - mid.astype(F32)).astype(BF16)
    return hi, mid, lo


def _dot_split3(a, sel):
    hi, mid, lo = _split3(a)
    return _dot(hi, sel) + _dot(mid, sel) + _dot(lo, sel)


def _pack_w_in(w):
    w = w.astype(BF16)
    o = _SRC
    xbc = o[8]
    pieces = [
        w[:, o[7]:o[8]],
        w[:, xbc:xbc + M2_INNER],
        w[:, o[10]:o[10] + D_MODEL],
        w[:, o[10] + D_MODEL:o[10] + 2 * D_MODEL],
        w[:, o[10] + 2 * D_MODEL:o[11]],
        w[:, o[0]:o[1]],
        w[:, o[3]:o[4]], w[:, o[4]:o[5]], w[:, o[5]:o[6]], w[:, o[6]:o[7]],
        w[:, xbc + M2_INNER:xbc + M2_INNER + M2_BC],
        w[:, xbc + M2_INNER + M2_BC:o[9]],
        w[:, o[1]:o[2]],
        w[:, o[2]:o[3]],
        w[:, o[9]:o[10]],
    ]
    packed = jnp.concatenate(pieces, axis=1)
    return jnp.pad(packed, ((0, 0), (0, P_DIM - packed.shape[1])))


def _inproj_body(x_ref, g_ref, w_ref, o_ref, h_ref):
    @pl.when(pl.program_id(1) == 0)
    def _():
        x = x_ref[...]
        ms = jnp.mean(x * x, axis=-1, keepdims=True)
        h_ref[...] = (x * lax.rsqrt(ms + EPS) * g_ref[...]).astype(BF16)

    o_ref[...] = _dot(h_ref[...], w_ref[...])


def _inproj(x2, g, w):
    n = x2.shape[0]
    tm = 1024 if n % 1024 == 0 else 512
    tn = 1024
    return pl.pallas_call(
        _inproj_body,
        name="inproj",
        grid=(n // tm, P_DIM // tn),
        in_specs=[pl.BlockSpec((tm, D_MODEL), lambda i, j: (i, 0)),
                  pl.BlockSpec((1, D_MODEL), lambda i, j: (0, 0)),
                  pl.BlockSpec((D_MODEL, tn), lambda i, j: (0, j))],
        out_specs=pl.BlockSpec((tm, tn), lambda i, j: (i, j)),
        out_shape=jax.ShapeDtypeStruct((n, P_DIM), F32),
        scratch_shapes=[pltpu.VMEM((tm, D_MODEL), BF16)],
        compiler_params=_cparams(("parallel", "arbitrary"), 48),
    )(x2, g, w)


def _head_sumsq(x, bd_ref):
    return _dot_split3(x * x, bd_ref[...])


def _rope(y, cos_t, sa_t, sb_t):
    w = y.shape[-1]
    return y * cos_t + pltpu.roll(y, w - ROPE_DIM // 2, 1) * sa_t + pltpu.roll(y, ROPE_DIM // 2, 1) * sb_t


def _ropetab_body(ang_ref, cos_o, sa_o, sb_o):
    half = ROPE_DIM // 2
    ang = ang_ref[...]
    lane = lax.broadcasted_iota(jnp.int32, ang.shape, 1) % NSA_HEAD_DIM
    sin_t = jnp.sin(ang)
    cos_o[...] = jnp.cos(ang)
    sa_o[...] = jnp.where(lane < half, -sin_t, 0.0)
    sb_o[...] = jnp.where((lane >= half) & (lane < ROPE_DIM), sin_t, 0.0)


def _ropetab(ang):
    n = ang.shape[0]
    t = 1024 if n % 1024 == 0 else 512
    blk = pl.BlockSpec((t, LANES), lambda i: (i, 0))
    sds = jax.ShapeDtypeStruct((n, LANES), F32)
    return pl.pallas_call(
        _ropetab_body,
        name="ropetab",
        grid=(n // t,),
        in_specs=[blk],
        out_specs=(blk, blk, blk),
        out_shape=(sds, sds, sds),
        compiler_params=_cparams(("parallel",), 32),
    )(ang)


def _nsaprep_body(q_ref, ksel_ref, vsel_ref, kwin_ref, vwin_ref, small_ref, cos_ref, sa_ref, sb_ref,
                  bdq_ref, bdk_ref, qg_ref, kg_ref,
                  qn_o, qr_o, ks_o, vs_o, kw_o, vw_o, gs_o):
    dh = NSA_HEAD_DIM
    cos_t = cos_ref[...]
    sa_t = sa_ref[...]
    sb_t = sb_ref[...]
    rep = NSA_WIDTH // LANES
    cos_q = jnp.concatenate([cos_t] * rep, axis=1)
    sa_q = jnp.concatenate([sa_t] * rep, axis=1)
    sb_q = jnp.concatenate([sb_t] * rep, axis=1)

    scale = NSA_Q_SCALE
    q = q_ref[...]
    qn = q * lax.rsqrt(_head_sumsq(q, bdq_ref) * (1.0 / dh) + EPS) * qg_ref[...]
    qr = _rope(qn, cos_q, sa_q, sb_q)
    tq = NSA_Q_BLOCK
    nrep = NSA_REP
    qn_t = (qn * scale).T
    qr_t = (qr * scale).T
    zrows = jnp.zeros((LANES - dh, nrep * tq), BF16)
    for g in range(NSA_KV_HEADS):
        for i in range(q.shape[0] // tq):
            def tile(a):
                return jnp.concatenate(
                    [a[(g * nrep + r) * dh:(g * nrep + r + 1) * dh, i * tq:(i + 1) * tq] for r in range(nrep)],
                    axis=1).astype(BF16)
            qn_o[0, g, i] = tile(qn_t)
            qr_o[0, g, i, 0:dh, :] = tile(qr_t)
            qr_o[0, g, i, dh:LANES, :] = zrows

    def knorm(k_ref, row):
        k = k_ref[...]
        kn = k * lax.rsqrt(_head_sumsq(k, bdk_ref) * (1.0 / dh) + EPS) * kg_ref[row:row + 1, :]
        return _rope(kn, cos_t, sa_t, sb_t).astype(BF16)

    ks = knorm(ksel_ref, 1)
    kw = knorm(kwin_ref, 2)
    vs_t = vsel_ref[...].T
    vw_t = vwin_ref[...].T
    zpad = jnp.zeros((q.shape[0], LANES - dh), BF16)
    sg = _sigmoid(small_ref[...])
    gate_rows = 4 * NSA_REP
    for g in range(NSA_KV_HEADS):
        ks_o[0, g] = jnp.concatenate([ks[:, g * dh:(g + 1) * dh], zpad], axis=1)
        kw_o[0, g] = jnp.concatenate([kw[:, g * dh:(g + 1) * dh], zpad], axis=1)
        vs_o[0, g] = vs_t[g * dh:(g + 1) * dh, :].astype(BF16)
        vw_o[0, g] = vw_t[g * dh:(g + 1) * dh, :].astype(BF16)
        sg_g = sg if g == 0 else pltpu.roll(sg, LANES - 3 * NSA_REP * g, 1)
        gs_o[0, g] = sg_g.T[0:gate_rows, :]


def _nsaprep(p, rope, qg, kg, b, s):
    t = 512
    nt = s // t
    dh = NSA_HEAD_DIM
    bdq = jnp.asarray(np.kron(np.eye(NSA_HEADS), np.ones((dh, dh))), BF16)
    bdk = jnp.asarray(np.kron(np.eye(NSA_KV_HEADS), np.ones((dh, dh))), BF16)
    qg_t = jnp.tile(qg, NSA_HEADS)[None, :]
    kg_t = jnp.tile(kg, (1, NSA_KV_HEADS))
    kvb = C_KV // LANES

    def col(width, cb):
        return pl.BlockSpec((t, width), lambda bi, i: (bi * nt + i, cb))

    def full(shape):
        return pl.BlockSpec(shape, lambda bi, i: (0,) * len(shape))

    g = NSA_KV_HEADS
    tq = NSA_Q_BLOCK
    cols = NSA_REP * tq
    gate_rows = 4 * NSA_REP

    def qt(rows):
        return (jax.ShapeDtypeStruct((b, g, s // tq, rows, cols), BF16),
                pl.BlockSpec((1, g, t // tq, rows, cols), lambda bi, i: (bi, 0, i, 0, 0)))

    def token_rows(w):
        return (jax.ShapeDtypeStruct((b, g, s, w), BF16),
                pl.BlockSpec((1, g, t, w), lambda bi, i: (bi, 0, i, 0)))

    def token_lanes(rows, dtype):
        return (jax.ShapeDtypeStruct((b, g, rows, s), dtype),
                pl.BlockSpec((1, g, rows, t), lambda bi, i: (bi, 0, 0, i)))

    outs = [qt(dh), qt(LANES), token_rows(LANES), token_lanes(dh, BF16), token_rows(LANES), token_lanes(dh, BF16),
            token_lanes(gate_rows, F32)]
    out_shape = tuple(o[0] for o in outs)
    return pl.pallas_call(
        _nsaprep_body,
        name="nsaprep",
        grid=(b, nt),
        in_specs=[col(NSA_WIDTH, C_Q // NSA_WIDTH),
                  col(LANES, kvb + 2), col(LANES, kvb + 3), col(LANES, kvb + 4), col(LANES, kvb + 5),
                  col(LANES, C_SMALL // LANES),
                  col(LANES, 0), col(LANES, 0), col(LANES, 0),
                  full((NSA_WIDTH, NSA_WIDTH)), full((LANES, LANES)),
                  full((1, NSA_WIDTH)), full((3, LANES))],
        out_specs=tuple(o[1] for o in outs),
        out_shape=out_shape,
        compiler_params=_cparams(("parallel", "parallel"), 40),
    )(p, p, p, p, p, p, rope[0], rope[1], rope[2], bdq, bdk, qg_t, kg_t)


def _compress_body(xk_ref, xv_ref, pk_ref, pv_ref, kw1_ref, kw2_ref, vw1_ref, vw2t_ref, kg_ref,
                   kc_o, vc_o):
    st = NSA_CMP_STRIDE
    nc = kc_o.shape[2]

    def hidden(x_ref, p_ref, w1_ref):
        top = bot = None
        for l in range(st):
            rows = x_ref[pl.ds(l, nc, stride=st), :].astype(BF16)
            t = _dot(rows, w1_ref[0, l * LANES:(l + 1) * LANES, :])
            u = _dot(rows, w1_ref[0, (st + l) * LANES:(st + l + 1) * LANES, :])
            top = t if top is None else top + t
            bot = u if bot is None else bot + u
        posb = _dot(p_ref[0], w1_ref[0])[0:1, :]
        hid = top + pltpu.roll(bot, nc - 1, 0) + posb
        return jax.nn.gelu(hid, approximate=True).astype(BF16)

    kc = _dot(hidden(xk_ref, pk_ref, kw1_ref), kw2_ref[...])
    ms = jnp.mean(kc * kc, axis=-1, keepdims=True)
    kc_o[0, 0] = (kc * lax.rsqrt(ms + EPS) * kg_ref[...]).astype(BF16)
    vc_o[0, 0] = _dot_nt(vw2t_ref[...], hidden(xv_ref, pv_ref, vw1_ref)).astype(BF16)


def _compress(p, pos_k, pos_v, kw1, kw2, vw1, vw2, kg0, b, s):
    dh = NSA_HEAD_DIM
    g = NSA_KV_HEADS
    nc = s // NSA_CMP_STRIDE
    kin = NSA_CMP_BLOCK * LANES
    kvb = C_KV // LANES

    def group_rows(a):
        a = a.reshape(NSA_CMP_BLOCK, dh, -1)
        return jnp.stack([jnp.pad(a, ((0, 0), (gi * dh, (g - 1 - gi) * dh), (0, 0))).reshape(kin, -1)
                          for gi in range(g)]).astype(BF16)

    def pos_rows(pe):
        return jnp.broadcast_to(jnp.swapaxes(group_rows(pe.reshape(-1, 1)), 1, 2), (g, SUBLANES, kin))

    def full(shape):
        return pl.BlockSpec(shape, lambda bi, gi: (0,) * len(shape))

    def per_group(shape):
        return pl.BlockSpec((1,) + shape, lambda bi, gi: (gi, 0, 0))

    oblk = pl.BlockSpec((1, 1, nc, dh), lambda bi, gi: (bi, gi, 0, 0))
    return pl.pallas_call(
        _compress_body,
        name="nsacompress",
        grid=(b, g),
        in_specs=[pl.BlockSpec((s, LANES), lambda bi, gi: (bi, kvb)),
                  pl.BlockSpec((s, LANES), lambda bi, gi: (bi, kvb + 1)),
                  per_group((SUBLANES, kin)), per_group((SUBLANES, kin)),
                  per_group((kin, NSA_CMP_HIDDEN)), full((NSA_CMP_HIDDEN, dh)),
                  per_group((kin, NSA_CMP_HIDDEN)), full((dh, NSA_CMP_HIDDEN)),
                  full((1, dh))],
        out_specs=(oblk, pl.BlockSpec((1, 1, dh, nc), lambda bi, gi: (bi, gi, 0, 0))),
        out_shape=(jax.ShapeDtypeStruct((b, g, nc, dh), BF16),
                   jax.ShapeDtypeStruct((b, g, dh, nc), BF16)),
        compiler_params=_cparams(("parallel", "arbitrary"), 48),
    )(p, p, pos_rows(pos_k), pos_rows(pos_v),
      group_rows(kw1), kw2.astype(BF16), group_rows(vw1), vw2.T.astype(BF16), kg0[None, :])


NSA_KT = 512


def _softmax_cols(s):
    m = jnp.max(s, axis=0, keepdims=True)
    e = jnp.exp2(s - m)
    l = jnp.sum(e, axis=0, keepdims=True)
    return e, jnp.where(m > 0.5 * NEG, 1.0 / l, 0.0)


def _nsa_body(qn_ref, qr_ref, kc_ref, vct_ref, ks_ref, vst_ref, kw_ref, vwt_ref, gs_ref, c2st_ref, et_ref,
              o_ref, m_ref, l_ref, acc_ref, sa_ref, sb_ref, *, top_n):
    tq = NSA_Q_BLOCK
    rep = NSA_REP
    dh = NSA_HEAD_DIM
    cols = rep * tq
    s0 = pl.program_id(2) * tq
    qn = qn_ref[0, 0, 0]
    qr = qr_ref[0, 0, 0]
    t_row = s0 + lax.broadcasted_iota(jnp.int32, (1, tq), 1)

    def heads(bias):
        return jnp.concatenate([bias] * rep, axis=1)

    nc = kc_ref.shape[2]
    cj = lax.broadcasted_iota(jnp.int32, (nc, 1), 0)
    cbias = jnp.where((cj * NSA_CMP_STRIDE + (NSA_CMP_BLOCK - 1)) <= t_row, 0.0, NEG)
    e, inv = _softmax_cols(_dot(kc_ref[0, 0], qn) + heads(cbias))
    p_cmp = e * inv
    o_cmp = _dot(vct_ref[0, 0], p_cmp.astype(BF16))

    psum = p_cmp[:, 0:tq]
    for r in range(1, rep):
        psum = psum + p_cmp[:, r * tq:(r + 1) * tq]
    c2st = c2st_ref[...]
    imp = sum(_dot(c2st, piece) for piece in _split3(psum))
    nsp = imp.shape[0]
    n_sel = ks_ref.shape[2] // NSA_SEL_BLOCK
    nb = lax.broadcasted_iota(jnp.int32, (nsp, 1), 0)
    cur = t_row // NSA_SEL_BLOCK
    forced = (nb == 0) | (nb == cur) | (nb == cur - 1)
    valid = nb * NSA_SEL_BLOCK <= t_row
    work = jnp.where(forced, NSA_BIG, jnp.where(valid, imp, -NSA_BIG))
    work = jnp.where(nb < n_sel, work, -jnp.inf)
    nbf = nb.astype(F32)
    sel_t = jnp.zeros((nsp, tq), F32)
    for _ in range(top_n):
        mx = jnp.max(work, axis=0, keepdims=True)
        idx = jnp.min(jnp.where(work == mx, nbf, float(nsp)), axis=0, keepdims=True)
        hit = nbf == idx
        sel_t = jnp.where(hit, 1.0, sel_t)
        work = jnp.where(hit, -jnp.inf, work)

    wlen = NSA_WINDOW + tq
    kstart = pl.multiple_of(jnp.maximum(s0 - NSA_WINDOW, 0), tq)
    kpos = kstart + lax.broadcasted_iota(jnp.int32, (wlen, 1), 0)
    wbias = jnp.where((kpos <= t_row) & ((t_row - kpos) < NSA_WINDOW), 0.0, NEG)
    ew, invw = _softmax_cols(_dot(kw_ref[0, 0, pl.ds(kstart, wlen), :], qr) + heads(wbias))
    o_win = _dot(vwt_ref[0, 0, :, pl.ds(kstart, wlen)], ew.astype(BF16)) * invw

    selbias = ((sel_t - 1.0) * -NEG).astype(BF16)
    qx = jnp.concatenate([heads(selbias), qr], axis=0)
    kt = NSA_KT
    m_ref[...] = jnp.full((1, cols), NEG, F32)
    l_ref[...] = jnp.zeros((1, cols), F32)
    acc_ref[...] = jnp.zeros((dh, cols), F32)

    def scores(j):
        k0 = pl.multiple_of(j * kt, kt)
        kx = jnp.concatenate([et_ref[pl.ds(k0, kt), :], ks_ref[0, 0, pl.ds(k0, kt), :]], axis=1)
        return _dot(kx, qx)

    def update(s, j):
        k0 = pl.multiple_of(j * kt, kt)
        m_old = m_ref[...]
        m_new = jnp.maximum(m_old, jnp.max(s, axis=0, keepdims=True))
        alpha = jnp.exp2(m_old - m_new)
        pe = jnp.exp2(s - m_new)
        l_ref[...] = alpha * l_ref[...] + jnp.sum(pe, axis=0, keepdims=True)
        acc_ref[...] = alpha * acc_ref[...] + _dot(vst_ref[0, 0, :, pl.ds(k0, kt)], pe.astype(BF16))
        m_ref[...] = m_new

    n_full = s0 // kt
    sa_ref[...] = scores(0)

    def tile_pair(jj, carry):
        j = 2 * jj
        sb_ref[...] = scores(j + 1)
        update(sa_ref[...], j)
        sa_ref[...] = scores(j + 2)
        update(sb_ref[...], j + 1)
        return carry

    lax.fori_loop(0, n_full // 2, tile_pair, 0)
    dcr = lax.broadcasted_iota(jnp.int32, (kt, tq), 0) - lax.broadcasted_iota(jnp.int32, (kt, tq), 1)
    dbias = heads(jnp.where(dcr <= s0 - n_full * kt, 0.0, NEG))

    @pl.when(n_full % 2 == 0)
    def _():
        update(sa_ref[...] + dbias, n_full)

    @pl.when(n_full % 2 == 1)
    def _():
        sb_ref[...] = scores(n_full)
        update(sa_ref[...], n_full - 1)
        update(sb_ref[...] + dbias, n_full)

    o_sel = acc_ref[...] * (1.0 / l_ref[...])

    gates = gs_ref[0, 0]

    def gate(c):
        return jnp.concatenate([gates[3 * r + c:3 * r + c + 1, :] for r in range(rep)], axis=1)

    o_t = gate(0) * o_cmp + gate(1) * o_sel + gate(2) * o_win
    pairs = [jnp.concatenate([o_t[:, r * tq:(r + 1) * tq], o_t[:, (r + 1) * tq:(r + 2) * tq]], axis=0).T
             for r in range(0, rep, 2)]
    o_ref[...] = jnp.concatenate(pairs, axis=1)


def _nsa_attention(qn, qr, kc, vc, ks, vs, kw, vw, gs, b, s):
    tq = NSA_Q_BLOCK
    dh = NSA_HEAD_DIM
    g = NSA_KV_HEADS
    nq = s // tq
    nc = s // NSA_CMP_STRIDE
    n_sel = s // NSA_SEL_BLOCK
    nsp = -(-n_sel // LANES) * LANES
    top_n = min(NSA_TOP_N, n_sel)
    c_start = np.arange(nc) * NSA_CMP_STRIDE
    s_start = np.arange(nsp) * NSA_SEL_BLOCK
    overlap = np.clip(np.minimum(c_start[:, None] + NSA_CMP_BLOCK, s_start[None, :] + NSA_SEL_BLOCK)
                      - np.maximum(c_start[:, None], s_start[None, :]), 0, None)
    c2st = jnp.asarray((overlap / NSA_CMP_BLOCK).T, BF16)
    et = jnp.asarray((np.arange(s) // NSA_SEL_BLOCK)[:, None] == np.arange(nsp)[None, :], BF16)
    cols = NSA_REP * tq
    gate_rows = gs.shape[2]

    def qblk(rows):
        return pl.BlockSpec((1, 1, 1, rows, cols), lambda bi, gi, i: (bi, gi, i, 0, 0))

    def whole(n, w=dh):
        return pl.BlockSpec((1, 1, n, w), lambda bi, gi, i: (bi, gi, 0, 0))

    def full(shape):
        return pl.BlockSpec(shape, lambda bi, gi, i: (0,) * len(shape))

    return pl.pallas_call(
        functools.partial(_nsa_body, top_n=top_n),
        name="nsaattn",
        grid=(b, g, nq),
        in_specs=[qblk(dh), qblk(LANES), whole(nc), whole(dh, nc), whole(s, LANES), whole(dh, s),
                  whole(s, LANES), whole(dh, s),
                  pl.BlockSpec((1, 1, gate_rows, tq), lambda bi, gi, i: (bi, gi, 0, i)),
                  full((nsp, nc)), full((s, nsp))],
        out_specs=pl.BlockSpec((tq, NSA_REP * dh), lambda bi, gi, i: (bi * nq + i, gi)),
        out_shape=jax.ShapeDtypeStruct((b * s, NSA_WIDTH), F32),
        scratch_shapes=[pltpu.VMEM((1, cols), F32), pltpu.VMEM((1, cols), F32), pltpu.VMEM((dh, cols), F32),
                        pltpu.VMEM((NSA_KT, cols), F32), pltpu.VMEM((NSA_KT, cols), F32)],
        compiler_params=_cparams(("parallel", "parallel", "arbitrary"), 56),
    )(qn, qr, kc, vc, ks, vs, kw, vw, gs, c2st, et)


HG_TILE = 256


def _hgrn_body(q_ref, f_ref, i_ref, g_ref, lbl_ref, ng_ref, tri_ref, o_ref,
               st_ref, gc_ref, k_ref, oacc_ref, *, layer):
    sub = HG_SUB
    dk = HG_HEAD_DIM
    th = q_ref.shape[0]

    @pl.when(pl.program_id(1) == 0)
    def _():
        st_ref[...] = jnp.zeros_like(st_ref)

    lbl = lbl_ref[...]
    el = jnp.exp(lbl - jnp.max(lbl, axis=0, keepdims=True))
    soft = el / jnp.sum(el, axis=0, keepdims=True)
    lb = jnp.zeros_like(soft[0:1])
    for d in range(1, layer + 1):
        lb = lb + soft[d:d + 1]

    f = lb + (1.0 - lb) * _sigmoid(f_ref[...])
    lf = jnp.log(f) * math.log2(math.e)
    k_ref[...] = 1.0 - f
    blk = tri_ref.shape[0]
    for c in range(th // blk):
        gc_ref[c * blk:(c + 1) * blk, :] = sum(
            _dot(tri_ref[...], piece) for piece in _split3(lf[c * blk:(c + 1) * blk, :]))

    srow = lax.broadcasted_iota(jnp.int32, (sub, 1), 0)

    def step(c, carry):
        r0 = pl.multiple_of(c * sub, sub)
        gall = gc_ref[pl.ds(r0, sub), :]
        qall = q_ref[pl.ds(r0, sub), :]
        kall = k_ref[pl.ds(r0, sub), :]
        vall = i_ref[pl.ds(r0, sub), :]
        outs = []
        for h in range(HG_HEADS):
            cs = slice(h * dk, (h + 1) * dk)
            g, q, kk, v = gall[:, cs], qall[:, cs], kall[:, cs], vall[:, cs]
            st = st_ref[h]
            o = _dot_nt((q * jnp.exp2(g)).astype(BF16), st.astype(BF16))
            for t in range(sub):
                d = jnp.where(srow <= t, g[t:t + 1, :] - g, NEG)
                w = (q[t:t + 1, :] * kk) * jnp.exp2(d)
                r = jnp.sum(w, axis=-1, keepdims=True)
                ot = jnp.sum(r * v, axis=0, keepdims=True)
                o = o + jnp.where(srow == t, ot, 0.0)
            g_last = g[sub - 1:sub, :]
            kt = kk * jnp.exp2(g_last - g)
            st_ref[h] = st * jnp.exp2(g_last) + _dot_tn(v.astype(BF16), kt.astype(BF16))
            outs.append(o)
        oacc_ref[pl.ds(r0, sub), :] = jnp.concatenate(outs, axis=1)
        return carry

    lax.fori_loop(0, th // sub, step, 0)

    o = oacc_ref[...]
    parts = []
    for h in range(HG_HEADS):
        oh = o[:, h * dk:(h + 1) * dk]
        ms = jnp.mean(oh * oh, axis=-1, keepdims=True)
        parts.append(oh * lax.rsqrt(ms + EPS) * ng_ref[...])
    o_ref[...] = jnp.concatenate(parts, axis=1) * _silu(g_ref[...])


def _hgrn(p, lb_logits, norm_g, layer, b, s):
    th = HG_TILE
    nt = s // th
    blk = 64
    tri = jnp.asarray(np.kron(np.eye(blk // HG_SUB), np.tril(np.ones((HG_SUB, HG_SUB)))), BF16)
    depth = lb_logits.shape[0]

    def col(cb):
        return pl.BlockSpec((th, HG_WIDTH), lambda bi, i: (bi * nt + i, cb))

    def full(shape):
        return pl.BlockSpec(shape, lambda bi, i: (0,) * len(shape))

    return pl.pallas_call(
        functools.partial(_hgrn_body, layer=layer),
        name="hgrn",
        grid=(b, nt),
        in_specs=[col(C_HQ // HG_WIDTH), col(C_HF // HG_WIDTH), col(C_HI // HG_WIDTH), col(C_HG // HG_WIDTH),
                  full((depth, HG_WIDTH)), full((1, HG_HEAD_DIM)), full((blk, blk))],
        out_specs=pl.BlockSpec((th, HG_WIDTH), lambda bi, i: (bi * nt + i, 0)),
        out_shape=jax.ShapeDtypeStruct((b * s, HG_WIDTH), F32),
        scratch_shapes=[pltpu.VMEM((HG_HEADS, HG_HEAD_DIM, HG_HEAD_DIM), F32),
                        pltpu.VMEM((th, HG_WIDTH), F32),
                        pltpu.VMEM((th, HG_WIDTH), F32),
                        pltpu.VMEM((th, HG_WIDTH), F32)],
        compiler_params=_cparams(("parallel", "arbitrary"), 40),
    )(p, p, p, p, lb_logits, norm_g[None, :], tri)


def _m2_body(z_ref, xs_ref, b_ref, c_ref, small_ref, cwx_ref, cwbc_ref, cbx_ref, cbbc_ref,
             dtb_ref, alog_ref, dskip_ref, ng_ref, tri_ref, trit_ref, e16_ref, o_ref,
             st_ref, px_ref, pbc_ref):
    ch = M2_CHUNK
    hp = M2_HEAD_DIM
    ns = M2_STATE
    halo = SUBLANES

    @pl.when(pl.program_id(1) == 0)
    def _():
        st_ref[...] = jnp.zeros_like(st_ref)
        px_ref[0:halo, :] = jnp.zeros((halo, M2_INNER), F32)
        pbc_ref[0:halo, :] = jnp.zeros((halo, 2 * M2_BC), F32)

    px_ref[halo:halo + ch, :] = xs_ref[...]
    pbc_ref[halo:halo + ch, 0:M2_BC] = b_ref[...]
    pbc_ref[halo:halo + ch, M2_BC:2 * M2_BC] = c_ref[...]

    def conv(p_ref, w_ref, bias_ref):
        full = p_ref[...]
        acc = bias_ref[...] + w_ref[M2_CONV - 1:M2_CONV, :] * full[halo:halo + ch, :]
        for back in range(1, M2_CONV):
            k = M2_CONV - 1 - back
            acc = acc + w_ref[k:k + 1, :] * pltpu.roll(full, back, 0)[halo:halo + ch, :]
        return _silu(acc)

    xs = conv(px_ref, cwx_ref, cbx_ref)
    bc = conv(pbc_ref, cwbc_ref, cbbc_ref)
    px_ref[0:halo, :] = px_ref[ch:ch + halo, :]
    pbc_ref[0:halo, :] = pbc_ref[ch:ch + halo, :]

    dtr = small_ref[...] + dtb_ref[...]
    dt = jnp.maximum(dtr, 0.0) + jnp.log(1.0 + jnp.exp(-jnp.abs(dtr)))
    a = dt * (-jnp.exp(alog_ref[...]))
    a3 = _split3(a)
    a_cs = sum(_dot(tri_ref[...], piece) for piece in a3)
    a_cs_t = sum(_dot_tn(piece, trit_ref[...]) for piece in a3)
    e16 = e16_ref[...]

    dt_x = _dot_split3(dt, e16)
    acs_x = _dot_split3(a_cs, e16)
    alast_x = acs_x[ch - 1:ch, :]
    xdt = xs * dt_x
    xdec = (xdt * jnp.exp(alast_x - acs_x)).astype(BF16)
    xdt_b = xdt.astype(BF16)
    dec_out_x = jnp.exp(acs_x)
    dec_chunk_x = jnp.exp(alast_x)

    li = lax.broadcasted_iota(jnp.int32, (ch, ch), 0)
    si = lax.broadcasted_iota(jnp.int32, (ch, ch), 1)
    tril = li >= si
    ys = []
    hpg = M2_HEADS // M2_GROUPS
    gw = hpg * hp
    for g in range(M2_GROUPS):
        gs_ = slice(g * gw, (g + 1) * gw)
        bm = bc[:, g * ns:(g + 1) * ns].astype(BF16)
        cm = bc[:, M2_BC + g * ns:M2_BC + (g + 1) * ns].astype(BF16)
        cb = _dot_nt(cm, bm)
        st = st_ref[g]
        ys.append(_dot(cm, st.astype(BF16)) * dec_out_x[:, gs_])
        st_ref[g] = st * dec_chunk_x[:, gs_] + _dot_tn(bm, xdec[:, gs_])
        for hh in range(hpg):
            h = g * hpg + hh
            hl = SMALL_DT + h
            seg = jnp.where(tril, a_cs[:, hl:hl + 1] - a_cs_t[hl:hl + 1, :], NEG)
            ys.append(_dot((cb * jnp.exp(seg)).astype(BF16), xdt_b[:, h * hp:(h + 1) * hp]))
    nd = 1 + hpg
    y_off = jnp.concatenate([ys[g * nd] for g in range(M2_GROUPS)], axis=1)
    y_diag = jnp.concatenate([ys[g * nd + 1 + hh] for g in range(M2_GROUPS) for hh in range(hpg)], axis=1)
    y = y_diag + y_off + dskip_ref[...] * xs
    y = y * _silu(z_ref[...])
    gw = M2_INNER // M2_GROUPS
    parts = []
    for g in range(M2_GROUPS):
        yg = y[:, g * gw:(g + 1) * gw]
        ms = jnp.mean(yg * yg, axis=-1, keepdims=True)
        parts.append(yg * lax.rsqrt(ms + EPS))
    o_ref[...] = jnp.concatenate(parts, axis=1) * ng_ref[...]


def _mamba2(p, conv_w, conv_b, dt_bias, a_log, d_skip, norm_g, b, s):
    ch = M2_CHUNK
    nt = s // ch
    tri = jnp.asarray(np.tril(np.ones((ch, ch))), BF16)
    spread = np.zeros((LANES, M2_INNER))
    spread[SMALL_DT:SMALL_DT + M2_HEADS] = np.kron(np.eye(M2_HEADS), np.ones((1, M2_HEAD_DIM)))
    e16 = jnp.asarray(spread, BF16)

    def lanes(v):
        return jnp.pad(v, (SMALL_DT, LANES - SMALL_DT - M2_HEADS))[None, :]

    def col(width, cb):
        return pl.BlockSpec((ch, width), lambda bi, i: (bi * nt + i, cb))

    def full(shape):
        return pl.BlockSpec(shape, lambda bi, i: (0,) * len(shape))

    return pl.pallas_call(
        _m2_body,
        name="mamba",
        grid=(b, nt),
        in_specs=[col(M2_INNER, C_Z // M2_INNER), col(M2_INNER, C_XS // M2_INNER),
                  col(M2_BC, C_B // M2_BC), col(M2_BC, C_C // M2_BC), col(LANES, C_SMALL // LANES),
                  full((M2_CONV, M2_INNER)), full((M2_CONV, 2 * M2_BC)),
                  full((1, M2_INNER)), full((1, 2 * M2_BC)),
                  full((1, LANES)), full((1, LANES)), full((1, M2_INNER)), full((1, M2_INNER)),
                  full((ch, ch)), full((ch, ch)), full((LANES, M2_INNER))],
        out_specs=pl.BlockSpec((ch, M2_INNER), lambda bi, i: (bi * nt + i, 0)),
        out_shape=jax.ShapeDtypeStruct((b * s, M2_INNER), F32),
        scratch_shapes=[pltpu.VMEM((M2_GROUPS, M2_STATE, M2_INNER // M2_GROUPS), F32),
                        pltpu.VMEM((ch + SUBLANES, M2_INNER), F32),
                        pltpu.VMEM((ch + SUBLANES, 2 * M2_BC), F32)],
        compiler_params=_cparams(("parallel", "arbitrary"), 40),
    )(p, p, p, p, p,
      conv_w[:, :M2_INNER], conv_w[:, M2_INNER:], conv_b[None, :M2_INNER], conv_b[None, M2_INNER:],
      lanes(dt_bias), lanes(a_log), jnp.repeat(d_skip, M2_HEAD_DIM)[None, :], norm_g[None, :], tri, tri.T, e16)


def _merge_body(x_ref, ya_ref, yb_ref, yc_ref, ga_ref, gb_ref, gc_ref, wa_ref, wb_ref, wc_ref, wo_ref, o_ref):
    merged = (_sigmoid(ga_ref[...]) * _dot(ya_ref[...].astype(BF16), wa_ref[...])
              + _sigmoid(gb_ref[...]) * _dot(yb_ref[...].astype(BF16), wb_ref[...])
              + _sigmoid(gc_ref[...]) * _dot(yc_ref[...].astype(BF16), wc_ref[...]))
    o_ref[...] = x_ref[...] + _dot(merged.astype(BF16), wo_ref[...])


def _merge(x2, ya, yb, yc, p, wa, wb, wc, wo):
    n = x2.shape[0]
    tm = 256

    def rows(width, cb=0):
        return pl.BlockSpec((tm, width), lambda i: (i, cb))

    def full(shape):
        return pl.BlockSpec(shape, lambda i: (0,) * len(shape))

    return pl.pallas_call(
        _merge_body,
        name="merge",
        grid=(n // tm,),
        in_specs=[rows(D_MODEL), rows(NSA_WIDTH), rows(HG_WIDTH), rows(M2_INNER),
                  rows(D_MODEL, C_GA // D_MODEL), rows(D_MODEL, C_GB // D_MODEL), rows(D_MODEL, C_GC // D_MODEL),
                  full((NSA_WIDTH, D_MODEL)), full((HG_WIDTH, D_MODEL)), full((M2_INNER, D_MODEL)),
                  full((D_MODEL, D_MODEL))],
        out_specs=rows(D_MODEL),
        out_shape=jax.ShapeDtypeStruct((n, D_MODEL), F32),
        compiler_params=_cparams(("parallel",), 48),
    )(x2, ya, yb, yc, p, p, p, wa.astype(BF16), wb.astype(BF16), wc.astype(BF16), wo.astype(BF16))


FFN_FT = 1408


def _ffn_body(x_ref, xh_ref, g_ref, wg_ref, wu_ref, cwg_ref, cwu_ref, cbg_ref, cbu_ref, wd_ref, o_ref,
              h_ref, ug_ref, uu_ref, acc_ref, *, tiles_per_seq):
    halo = SUBLANES
    tm = x_ref.shape[0]
    j = pl.program_id(1)

    def norm(x):
        ms = jnp.mean(x * x, axis=-1, keepdims=True)
        return (x * lax.rsqrt(ms + EPS) * g_ref[...]).astype(BF16)

    @pl.when(j == 0)
    def _():
        first = (pl.program_id(0) % tiles_per_seq) == 0
        h_ref[0:halo, :] = jnp.where(first, jnp.zeros((halo, D_MODEL), BF16), norm(xh_ref[...]))
        h_ref[halo:halo + tm, :] = norm(x_ref[...])
        acc_ref[...] = jnp.zeros_like(acc_ref)

    h = h_ref[...]
    ug_ref[...] = _dot(h, wg_ref[...])
    uu_ref[...] = _dot(h, wu_ref[...])

    def conv(u_ref, w_ref, bias_ref):
        acc = bias_ref[...]
        for k in range(FFN_CONV):
            off = halo - (FFN_CONV - 1) + k
            acc = acc + w_ref[k:k + 1, :] * u_ref[off:off + tm, :]
        return acc

    act = _silu(conv(ug_ref, cwg_ref, cbg_ref)) * conv(uu_ref, cwu_ref, cbu_ref)
    acc_ref[...] += _dot(act.astype(BF16), wd_ref[...])

    @pl.when(j == pl.num_programs(1) - 1)
    def _():
        o_ref[...] = x_ref[...] + acc_ref[...]


def _conv_ffn(x2, g, w_up, conv_w, conv_b, w_down, s):
    n = x2.shape[0]
    tm = 512
    ft = FFN_FT
    nf = FFN_DIM // ft
    halo = SUBLANES
    hb = tm // halo
    w_up = w_up.astype(BF16)
    return pl.pallas_call(
        functools.partial(_ffn_body, tiles_per_seq=s // tm),
        name="convffn",
        grid=(n // tm, nf),
        in_specs=[pl.BlockSpec((tm, D_MODEL), lambda i, j: (i, 0)),
                  pl.BlockSpec((halo, D_MODEL), lambda i, j: (jnp.maximum(i * hb - 1, 0), 0)),
                  pl.BlockSpec((1, D_MODEL), lambda i, j: (0, 0)),
                  pl.BlockSpec((D_MODEL, ft), lambda i, j: (0, j)),
                  pl.BlockSpec((D_MODEL, ft), lambda i, j: (0, nf + j)),
                  pl.BlockSpec((FFN_CONV, ft), lambda i, j: (0, j)),
                  pl.BlockSpec((FFN_CONV, ft), lambda i, j: (0, nf + j)),
                  pl.BlockSpec((1, ft), lambda i, j: (0, j)),
                  pl.BlockSpec((1, ft), lambda i, j: (0, nf + j)),
                  pl.BlockSpec((ft, D_MODEL), lambda i, j: (j, 0))],
        out_specs=pl.BlockSpec((tm, D_MODEL), lambda i, j: (i, 0)),
        out_shape=jax.ShapeDtypeStruct((n, D_MODEL), F32),
        scratch_shapes=[pltpu.VMEM((tm + halo, D_MODEL), BF16),
                        pltpu.VMEM((tm + halo, ft), F32),
                        pltpu.VMEM((tm + halo, ft), F32),
                        pltpu.VMEM((tm, D_MODEL), F32)],
        compiler_params=_cparams(("parallel", "arbitrary"), 56),
    )(x2, x2, g, w_up, w_up, conv_w, conv_w, conv_b[None, :], conv_b[None, :], w_down.astype(BF16))


def _rope_angles(positions):
    half = ROPE_DIM // 2
    inv_freq = ROPE_THETA ** (-jnp.arange(0, ROPE_DIM, 2, dtype=F32) / ROPE_DIM)
    lane = np.arange(LANES) % NSA_HEAD_DIM
    freq = jnp.where(jnp.asarray(lane < ROPE_DIM), inv_freq[jnp.asarray(lane % half)], 0.0)
    return positions.astype(F32).reshape(-1, 1) * freq[None, :]


def kernel(x, positions, attn_norm_g, ffn_norm_g, w_in, nsa_q_norm_g, nsa_k_norm_g, nsa_cmp_pos_k, nsa_cmp_pos_v, nsa_cmp_k_w1, nsa_cmp_k_w2, nsa_cmp_v_w1, nsa_cmp_v_w2, hgrn_lb_logits, hgrn_norm_g, m2_conv_w, m2_conv_b, m2_dt_bias, m2_a_log, m2_d_skip, m2_norm_g, w_branch_nsa, w_branch_hgrn, w_branch_m2, w_out, ffn_w_up, ffn_conv_w, ffn_conv_b, ffn_w_down):
    b, s, _ = x.shape
    depth = w_in.shape[0]
    x2 = x.reshape(b * s, D_MODEL)
    rope = _ropetab(_rope_angles(positions))
    for l in range(depth):
        p = _inproj(x2, attn_norm_g[l][None, :], _pack_w_in(w_in[l]))
        qn, qr, ks, vs, kw, vw, gs = _nsaprep(p, rope, nsa_q_norm_g[l], nsa_k_norm_g[l], b, s)
        kc, vc = _compress(p, nsa_cmp_pos_k[l], nsa_cmp_pos_v[l], nsa_cmp_k_w1[l], nsa_cmp_k_w2[l],
                           nsa_cmp_v_w1[l], nsa_cmp_v_w2[l], nsa_k_norm_g[l, 0], b, s)
        ya = _nsa_attention(qn, qr, kc, vc, ks, vs, kw, vw, gs, b, s)
        yb = _hgrn(p, hgrn_lb_logits, hgrn_norm_g[l], l, b, s)
        yc = _mamba2(p, m2_conv_w[l], m2_conv_b[l], m2_dt_bias[l], m2_a_log[l], m2_d_skip[l], m2_norm_g[l], b, s)
        x2 = _merge(x2, ya, yb, yc, p, w_branch_nsa[l], w_branch_hgrn[l], w_branch_m2[l], w_out[l])
        x2 = _conv_ffn(x2, ffn_norm_g[l][None, :], ffn_w_up[l], ffn_conv_w[l], ffn_conv_b[l], ffn_w_down[l], s)
    return x2.reshape(b, s, D_MODEL)
```

```python
import functools
import math

import numpy as np
import jax
import jax.numpy as jnp
from jax import lax
from jax.experimental import pallas as pl
from jax.experimental.pallas import tpu as pltpu

F32 = jnp.float32
BF16 = jnp.bfloat16

D_MODEL = 1024
NSA_HEADS = 8
NSA_KV_HEADS = 2
NSA_REP = NSA_HEADS // NSA_KV_HEADS
NSA_HEAD_DIM = 64
NSA_CMP_BLOCK = 32
NSA_CMP_STRIDE = 16
NSA_SEL_BLOCK = 64
NSA_TOP_N = 16
NSA_WINDOW = 512
NSA_CMP_HIDDEN = 256
NSA_Q_BLOCK = 256
NSA_BIG = 1e9
ROPE_THETA = 500000.0
ROPE_DIM = NSA_HEAD_DIM // 4
NSA_WIDTH = NSA_HEADS * NSA_HEAD_DIM
NSA_KV_WIDTH = NSA_KV_HEADS * NSA_HEAD_DIM
HG_HEADS = 4
HG_HEAD_DIM = 128
HG_WIDTH = HG_HEADS * HG_HEAD_DIM
HG_SUB = 16
M2_HEADS = 16
M2_HEAD_DIM = 64
M2_INNER = M2_HEADS * M2_HEAD_DIM
M2_GROUPS = 2
M2_STATE = 128
M2_CONV = 4
M2_CHUNK = 128
M2_BC = M2_GROUPS * M2_STATE
FFN_DIM = 2816
FFN_CONV = 3
EPS = 1e-6

NEG = -1e30
NSA_Q_SCALE = NSA_HEAD_DIM ** -0.5 * math.log2(math.e)
LANES = 128
SUBLANES = 8

C_Z = 0
C_XS = 1024
C_GA = 2048
C_GB = 3072
C_GC = 4096
C_Q = 5120
C_HQ = 5632
C_HF = 6144
C_HI = 6656
C_HG = 7168
C_B = 7680
C_C = 7936
C_KV = 8192
C_SMALL = 8960
P_DIM = 9216
SMALL_DT = 3 * NSA_HEADS
P_TILE = 1024
F32_TILES = (C_HF // P_TILE, C_KV // P_TILE)
F_HF = 0
F_HI = C_HI - C_HF
F_KV = P_TILE
F_SMALL = P_TILE + C_SMALL - C_KV
F_DIM = 2 * P_TILE

_SRC = np.cumsum([0, NSA_WIDTH, 6 * NSA_KV_WIDTH, 3 * NSA_HEADS, HG_WIDTH, HG_WIDTH, HG_WIDTH, HG_WIDTH,
                  M2_INNER, M2_INNER + 2 * M2_BC, M2_HEADS, 3 * D_MODEL]).tolist()


def _cparams(sem, vmem_mib):
    return pltpu.CompilerParams(dimension_semantics=sem, vmem_limit_bytes=vmem_mib * 1024 * 1024)


def _sigmoid(x):
    return 1.0 / (1.0 + jnp.exp(-x))


def _silu(x):
    return x * _sigmoid(x)


def _dot(a, b):
    return jnp.dot(a, b, preferred_element_type=F32)


def _dot_nt(a, b):
    return lax.dot_general(a, b, (((1,), (1,)), ((), ())), preferred_element_type=F32)


def _dot_tn(a, b):
    return lax.dot_general(a, b, (((0,), (0,)), ((), ())), preferred_element_type=F32)


def _split3(a):
    hi = a.astype(BF16)
    r1 = a - hi.astype(F32)
    mid = r1.astype(BF16)
    lo = (r1 - mid.astype(F32)).astype(BF16)
    return hi, mid, lo


def _dot_split3(a, sel):
    hi, mid, lo = _split3(a)
    return _dot(hi, sel) + _dot(mid, sel) + _dot(lo, sel)


def _pack_w_in(w):
    w = w.astype(BF16)
    o = _SRC
    xbc = o[8]
    pieces = [
        w[:, o[7]:o[8]],
        w[:, xbc:xbc + M2_INNER],
        w[:, o[10]:o[10] + D_MODEL],
        w[:, o[10] + D_MODEL:o[10] + 2 * D_MODEL],
        w[:, o[10] + 2 * D_MODEL:o[11]],
        w[:, o[0]:o[1]],
        w[:, o[3]:o[4]], w[:, o[4]:o[5]], w[:, o[5]:o[6]], w[:, o[6]:o[7]],
        w[:, xbc + M2_INNER:xbc + M2_INNER + M2_BC],
        w[:, xbc + M2_INNER + M2_BC:o[9]],
        w[:, o[1]:o[2]],
        w[:, o[2]:o[3]],
        w[:, o[9]:o[10]],
    ]
    packed = jnp.concatenate(pieces, axis=1)
    return jnp.pad(packed, ((0, 0), (0, P_DIM - packed.shape[1])))


def _inproj_body(x_ref, g_ref, w_ref, o_ref, of_ref, h_ref):
    j = pl.program_id(1)

    @pl.when(j == 0)
    def _():
        x = x_ref[...]
        ms = jnp.mean(x * x, axis=-1, keepdims=True)
        h_ref[...] = (x * lax.rsqrt(ms + EPS) * g_ref[...]).astype(BF16)

    acc = _dot(h_ref[...], w_ref[...])
    o_ref[...] = acc.astype(BF16)

    @pl.when((j == F32_TILES[0]) | (j == F32_TILES[1]))
    def _():
        of_ref[...] = acc


def _inproj(x2, g, w):
    n = x2.shape[0]
    tm = 1024 if n % 1024 == 0 else 512
    tn = P_TILE
    return pl.pallas_call(
        _inproj_body,
        name="inproj",
        grid=(n // tm, P_DIM // tn),
        in_specs=[pl.BlockSpec((tm, D_MODEL), lambda i, j: (i, 0)),
                  pl.BlockSpec((1, D_MODEL), lambda i, j: (0, 0)),
                  pl.BlockSpec((D_MODEL, tn), lambda i, j: (0, j))],
        out_specs=(pl.BlockSpec((tm, tn), lambda i, j: (i, j)),
                   pl.BlockSpec((tm, tn), lambda i, j: (i, jnp.where(j <= F32_TILES[0], 0, 1)))),
        out_shape=(jax.ShapeDtypeStruct((n, P_DIM), BF16), jax.ShapeDtypeStruct((n, F_DIM), F32)),
        scratch_shapes=[pltpu.VMEM((tm, D_MODEL), BF16)],
        compiler_params=_cparams(("parallel", "arbitrary"), 48),
    )(x2, g, w)


def _head_sumsq(x, bd_ref):
    return _dot_split3(x * x, bd_ref[...])


def _rope(y, cos_t, sa_t, sb_t):
    w = y.shape[-1]
    return y * cos_t + pltpu.roll(y, w - ROPE_DIM // 2, 1) * sa_t + pltpu.roll(y, ROPE_DIM // 2, 1) * sb_t


def _ropetab_body(ang_ref, cos_o, sa_o, sb_o):
    half = ROPE_DIM // 2
    ang = ang_ref[...]
    lane = lax.broadcasted_iota(jnp.int32, ang.shape, 1) % NSA_HEAD_DIM
    sin_t = jnp.sin(ang)
    cos_o[...] = jnp.cos(ang)
    sa_o[...] = jnp.where(lane < half, -sin_t, 0.0)
    sb_o[...] = jnp.where((lane >= half) & (lane < ROPE_DIM), sin_t, 0.0)


def _ropetab(ang):
    n = ang.shape[0]
    t = 1024 if n % 1024 == 0 else 512
    blk = pl.BlockSpec((t, LANES), lambda i: (i, 0))
    sds = jax.ShapeDtypeStruct((n, LANES), F32)
    return pl.pallas_call(
        _ropetab_body,
        name="ropetab",
        grid=(n // t,),
        in_specs=[blk],
        out_specs=(blk, blk, blk),
        out_shape=(sds, sds, sds),
        compiler_params=_cparams(("parallel",), 32),
    )(ang)


def _nsaprep_body(q_ref, ksel_ref, vsel_ref, kwin_ref, vwin_ref, small_ref, cos_ref, sa_ref, sb_ref,
                  bdq_ref, bdk_ref, qg_ref, kg_ref,
                  qn_o, qr_o, ks_o, vs_o, kw_o, vw_o, gs_o):
    dh = NSA_HEAD_DIM
    cos_t = cos_ref[...]
    sa_t = sa_ref[...]
    sb_t = sb_ref[...]
    rep = NSA_WIDTH // LANES
    cos_q = jnp.concatenate([cos_t] * rep, axis=1)
    sa_q = jnp.concatenate([sa_t] * rep, axis=1)
    sb_q = jnp.concatenate([sb_t] * rep, axis=1)

    scale = NSA_Q_SCALE
    q = q_ref[...].astype(F32)
    qn = q * lax.rsqrt(_head_sumsq(q, bdq_ref) * (1.0 / dh) + EPS) * qg_ref[...]
    qr = _rope(qn, cos_q, sa_q, sb_q)
    tq = NSA_Q_BLOCK
    nrep = NSA_REP
    qn_t = (qn * scale).T
    qr_t = (qr * scale).T
    zrows = jnp.zeros((LANES - dh, nrep * tq), BF16)
    for g in range(NSA_KV_HEADS):
        for i in range(q.shape[0] // tq):
            def tile(a):
                return jnp.concatenate(
                    [a[(g * nrep + r) * dh:(g * nrep + r + 1) * dh, i * tq:(i + 1) * tq] for r in range(nrep)],
                    axis=1).astype(BF16)
            qn_o[0, g, i] = tile(qn_t)
            qr_o[0, g, i, 0:dh, :] = tile(qr_t)
            qr_o[0, g, i, dh:LANES, :] = zrows

    def knorm(k_ref, row):
        k = k_ref[...]
        kn = k * lax.rsqrt(_head_sumsq(k, bdk_ref) * (1.0 / dh) + EPS) * kg_ref[row:row + 1, :]
        return _rope(kn, cos_t, sa_t, sb_t).astype(BF16)

    ks = knorm(ksel_ref, 1)
    kw = knorm(kwin_ref, 2)
    vs_t = vsel_ref[...].T
    vw_t = vwin_ref[...].T
    zpad = jnp.zeros((q.shape[0], LANES - dh), BF16)
    sg = _sigmoid(small_ref[...])
    gate_rows = 4 * NSA_REP
    for g in range(NSA_KV_HEADS):
        ks_o[0, g] = jnp.concatenate([ks[:, g * dh:(g + 1) * dh], zpad], axis=1)
        kw_o[0, g] = jnp.concatenate([kw[:, g * dh:(g + 1) * dh], zpad], axis=1)
        vs_o[0, g] = vs_t[g * dh:(g + 1) * dh, :].astype(BF16)
        vw_o[0, g] = vw_t[g * dh:(g + 1) * dh, :].astype(BF16)
        sg_g = sg if g == 0 else pltpu.roll(sg, LANES - 3 * NSA_REP * g, 1)
        gs_o[0, g] = sg_g.T[0:gate_rows, :]


def _nsaprep(p, pf, rope, qg, kg, b, s):
    t = 512
    nt = s // t
    dh = NSA_HEAD_DIM
    bdq = jnp.asarray(np.kron(np.eye(NSA_HEADS), np.ones((dh, dh))), BF16)
    bdk = jnp.asarray(np.kron(np.eye(NSA_KV_HEADS), np.ones((dh, dh))), BF16)
    qg_t = jnp.tile(qg, NSA_HEADS)[None, :]
    kg_t = jnp.tile(kg, (1, NSA_KV_HEADS))
    kvb = F_KV // LANES

    def col(width, cb):
        return pl.BlockSpec((t, width), lambda bi, i: (bi * nt + i, cb))

    def full(shape):
        return pl.BlockSpec(shape, lambda bi, i: (0,) * len(shape))

    g = NSA_KV_HEADS
    tq = NSA_Q_BLOCK
    cols = NSA_REP * tq
    gate_rows = 4 * NSA_REP

    def qt(rows):
        return (jax.ShapeDtypeStruct((b, g, s // tq, rows, cols), BF16),
                pl.BlockSpec((1, g, t // tq, rows, cols), lambda bi, i: (bi, 0, i, 0, 0)))

    def token_rows(w):
        return (jax.ShapeDtypeStruct((b, g, s, w), BF16),
                pl.BlockSpec((1, g, t, w), lambda bi, i: (bi, 0, i, 0)))

    def token_lanes(rows, dtype):
        return (jax.ShapeDtypeStruct((b, g, rows, s), dtype),
                pl.BlockSpec((1, g, rows, t), lambda bi, i: (bi, 0, 0, i)))

    outs = [qt(dh), qt(LANES), token_rows(LANES), token_lanes(dh, BF16), token_rows(LANES), token_lanes(dh, BF16),
            token_lanes(gate_rows, F32)]
    out_shape = tuple(o[0] for o in outs)
    return pl.pallas_call(
        _nsaprep_body,
        name="nsaprep",
        grid=(b, nt),
        in_specs=[col(NSA_WIDTH, C_Q // NSA_WIDTH),
                  col(LANES, kvb + 2), col(LANES, kvb + 3), col(LANES, kvb + 4), col(LANES, kvb + 5),
                  col(LANES, F_SMALL // LANES),
                  col(LANES, 0), col(LANES, 0), col(LANES, 0),
                  full((NSA_WIDTH, NSA_WIDTH)), full((LANES, LANES)),
                  full((1, NSA_WIDTH)), full((3, LANES))],
        out_specs=tuple(o[1] for o in outs),
        out_shape=out_shape,
        compiler_params=_cparams(("parallel", "parallel"), 40),
    )(p, pf, pf, pf, pf, pf, rope[0], rope[1], rope[2], bdq, bdk, qg_t, kg_t)


def _compress_body(xk_ref, xv_ref, pk_ref, pv_ref, kw1_ref, kw2_ref, vw1_ref, vw2t_ref, kg_ref,
                   kc_o, vc_o):
    st = NSA_CMP_STRIDE
    nc = kc_o.shape[2]

    def hidden(x_ref, p_ref, w1_ref):
        top = bot = None
        for l in range(st):
            rows = x_ref[pl.ds(l, nc, stride=st), :].astype(BF16)
            t = _dot(rows, w1_ref[0, l * LANES:(l + 1) * LANES, :])
            u = _dot(rows, w1_ref[0, (st + l) * LANES:(st + l + 1) * LANES, :])
            top = t if top is None else top + t
            bot = u if bot is None else bot + u
        posb = _dot(p_ref[0], w1_ref[0])[0:1, :]
        hid = top + pltpu.roll(bot, nc - 1, 0) + posb
        return jax.nn.gelu(hid, approximate=True).astype(BF16)

    kc = _dot(hidden(xk_ref, pk_ref, kw1_ref), kw2_ref[...])
    ms = jnp.mean(kc * kc, axis=-1, keepdims=True)
    kc_o[0, 0] = (kc * lax.rsqrt(ms + EPS) * kg_ref[...]).astype(BF16)
    vc_o[0, 0] = _dot_nt(vw2t_ref[...], hidden(xv_ref, pv_ref, vw1_ref)).astype(BF16)


def _compress(p, pos_k, pos_v, kw1, kw2, vw1, vw2, kg0, b, s):
    dh = NSA_HEAD_DIM
    g = NSA_KV_HEADS
    nc = s // NSA_CMP_STRIDE
    kin = NSA_CMP_BLOCK * LANES
    kvb = F_KV // LANES

    def group_rows(a):
        a = a.reshape(NSA_CMP_BLOCK, dh, -1)
        return jnp.stack([jnp.pad(a, ((0, 0), (gi * dh, (g - 1 - gi) * dh), (0, 0))).reshape(kin, -1)
                          for gi in range(g)]).astype(BF16)

    def pos_rows(pe):
        return jnp.broadcast_to(jnp.swapaxes(group_rows(pe.reshape(-1, 1)), 1, 2), (g, SUBLANES, kin))

    def full(shape):
        return pl.BlockSpec(shape, lambda bi, gi: (0,) * len(shape))

    def per_group(shape):
        return pl.BlockSpec((1,) + shape, lambda bi, gi: (gi, 0, 0))

    oblk = pl.BlockSpec((1, 1, nc, dh), lambda bi, gi: (bi, gi, 0, 0))
    return pl.pallas_call(
        _compress_body,
        name="nsacompress",
        grid=(b, g),
        in_specs=[pl.BlockSpec((s, LANES), lambda bi, gi: (bi, kvb)),
                  pl.BlockSpec((s, LANES), lambda bi, gi: (bi, kvb + 1)),
                  per_group((SUBLANES, kin)), per_group((SUBLANES, kin)),
                  per_group((kin, NSA_CMP_HIDDEN)), full((NSA_CMP_HIDDEN, dh)),
                  per_group((kin, NSA_CMP_HIDDEN)), full((dh, NSA_CMP_HIDDEN)),
                  full((1, dh))],
        out_specs=(oblk, pl.BlockSpec((1, 1, dh, nc), lambda bi, gi: (bi, gi, 0, 0))),
        out_shape=(jax.ShapeDtypeStruct((b, g, nc, dh), BF16),
                   jax.ShapeDtypeStruct((b, g, dh, nc), BF16)),
        compiler_params=_cparams(("parallel", "arbitrary"), 48),
    )(p, p, pos_rows(pos_k), pos_rows(pos_v),
      group_rows(kw1), kw2.astype(BF16), group_rows(vw1), vw2.T.astype(BF16), kg0[None, :])


NSA_KT = 512


def _softmax_cols(s):
    m = jnp.max(s, axis=0, keepdims=True)
    e = jnp.exp2(s - m)
    l = jnp.sum(e, axis=0, keepdims=True)
    return e, jnp.where(m > 0.5 * NEG, 1.0 / l, 0.0)


def _nsa_body(qn_ref, qr_ref, kc_ref, vct_ref, ks_ref, vst_ref, kw_ref, vwt_ref, gs_ref, c2st_ref, et_ref,
              o_ref, m_ref, l_ref, acc_ref, sa_ref, sb_ref, *, top_n):
    tq = NSA_Q_BLOCK
    rep = NSA_REP
    dh = NSA_HEAD_DIM
    cols = rep * tq
    s0 = pl.program_id(2) * tq
    qn = qn_ref[0, 0, 0]
    qr = qr_ref[0, 0, 0]
    t_row = s0 + lax.broadcasted_iota(jnp.int32, (1, tq), 1)

    def heads(bias):
        return jnp.concatenate([bias] * rep, axis=1)

    nc = kc_ref.shape[2]
    cj = lax.broadcasted_iota(jnp.int32, (nc, 1), 0)
    cbias = jnp.where((cj * NSA_CMP_STRIDE + (NSA_CMP_BLOCK - 1)) <= t_row, 0.0, NEG)
    e, inv = _softmax_cols(_dot(kc_ref[0, 0], qn) + heads(cbias))
    p_cmp = e * inv
    o_cmp = _dot(vct_ref[0, 0], p_cmp.astype(BF16))

    psum = p_cmp[:, 0:tq]
    for r in range(1, rep):
        psum = psum + p_cmp[:, r * tq:(r + 1) * tq]
    c2st = c2st_ref[...]
    imp = sum(_dot(c2st, piece) for piece in _split3(psum))
    nsp = imp.shape[0]
    n_sel = ks_ref.shape[2] // NSA_SEL_BLOCK
    nb = lax.broadcasted_iota(jnp.int32, (nsp, 1), 0)
    cur = t_row // NSA_SEL_BLOCK
    forced = (nb == 0) | (nb == cur) | (nb == cur - 1)
    valid = nb * NSA_SEL_BLOCK <= t_row
    work = jnp.where(forced, NSA_BIG, jnp.where(valid, imp, -NSA_BIG))
    work = jnp.where(nb < n_sel, work, -jnp.inf)
    nbf = nb.astype(F32)
    for _ in range(top_n):
        mx = jnp.max(work, axis=0, keepdims=True)
        idx = jnp.min(jnp.where(work == mx, nbf, float(nsp)), axis=0, keepdims=True)
        work = jnp.where(nbf == idx, -jnp.inf, work)
    picked = (work == -jnp.inf) & (nb < n_sel)

    wlen = NSA_WINDOW + tq
    kstart = pl.multiple_of(jnp.maximum(s0 - NSA_WINDOW, 0), tq)
    kpos = kstart + lax.broadcasted_iota(jnp.int32, (wlen, 1), 0)
    wbias = jnp.where((kpos <= t_row) & ((t_row - kpos) < NSA_WINDOW), 0.0, NEG)
    ew, invw = _softmax_cols(_dot(kw_ref[0, 0, pl.ds(kstart, wlen), :], qr) + heads(wbias))
    o_win = _dot(vwt_ref[0, 0, :, pl.ds(kstart, wlen)], ew.astype(BF16)) * invw

    selbias = jnp.where(picked, 0.0, NEG).astype(BF16)
    qx = jnp.concatenate([heads(selbias), qr], axis=0)
    kt = NSA_KT
    m_ref[...] = jnp.full((1, cols), NEG, F32)
    l_ref[...] = jnp.zeros((1, cols), F32)
    acc_ref[...] = jnp.zeros((dh, cols), F32)

    def scores(j):
        k0 = pl.multiple_of(j * kt, kt)
        kx = jnp.concatenate([et_ref[pl.ds(k0, kt), :], ks_ref[0, 0, pl.ds(k0, kt), :]], axis=1)
        return _dot(kx, qx)

    def update(s, j):
        k0 = pl.multiple_of(j * kt, kt)
        m_old = m_ref[...]
        m_new = jnp.maximum(m_old, jnp.max(s, axis=0, keepdims=True))
        alpha = jnp.exp2(m_old - m_new)
        pe = jnp.exp2(s - m_new)
        l_ref[...] = alpha * l_ref[...] + jnp.sum(pe, axis=0, keepdims=True)
        acc_ref[...] = alpha * acc_ref[...] + _dot(vst_ref[0, 0, :, pl.ds(k0, kt)], pe.astype(BF16))
        m_ref[...] = m_new

    n_full = s0 // kt
    sa_ref[...] = scores(0)

    def tile_pair(jj, carry):
        j = 2 * jj
        sb_ref[...] = scores(j + 1)
        update(sa_ref[...], j)
        sa_ref[...] = scores(j + 2)
        update(sb_ref[...], j + 1)
        return carry

    lax.fori_loop(0, n_full // 2, tile_pair, 0)
    dcr = lax.broadcasted_iota(jnp.int32, (kt, tq), 0) - lax.broadcasted_iota(jnp.int32, (kt, tq), 1)
    dbias = heads(jnp.where(dcr <= s0 - n_full * kt, 0.0, NEG))

    @pl.when(n_full % 2 == 0)
    def _():
        update(sa_ref[...] + dbias, n_full)

    @pl.when(n_full % 2 == 1)
    def _():
        sb_ref[...] = scores(n_full)
        update(sa_ref[...], n_full - 1)
        update(sb_ref[...] + dbias, n_full)

    o_sel = acc_ref[...] * (1.0 / l_ref[...])

    gates = gs_ref[0, 0]

    def gate(c):
        return jnp.concatenate([gates[3 * r + c:3 * r + c + 1, :] for r in range(rep)], axis=1)

    o_t = gate(0) * o_cmp + gate(1) * o_sel + gate(2) * o_win
    pairs = [jnp.concatenate([o_t[:, r * tq:(r + 1) * tq], o_t[:, (r + 1) * tq:(r + 2) * tq]], axis=0).T
             for r in range(0, rep, 2)]
    o_ref[...] = jnp.concatenate(pairs, axis=1).astype(o_ref.dtype)


def _nsa_attention(qn, qr, kc, vc, ks, vs, kw, vw, gs, b, s):
    tq = NSA_Q_BLOCK
    dh = NSA_HEAD_DIM
    g = NSA_KV_HEADS
    nq = s // tq
    nc = s // NSA_CMP_STRIDE
    n_sel = s // NSA_SEL_BLOCK
    nsp = -(-n_sel // LANES) * LANES
    top_n = min(NSA_TOP_N, n_sel)
    c_start = np.arange(nc) * NSA_CMP_STRIDE
    s_start = np.arange(nsp) * NSA_SEL_BLOCK
    overlap = np.clip(np.minimum(c_start[:, None] + NSA_CMP_BLOCK, s_start[None, :] + NSA_SEL_BLOCK)
                      - np.maximum(c_start[:, None], s_start[None, :]), 0, None)
    c2st = jnp.asarray((overlap / NSA_CMP_BLOCK).T, BF16)
    et = jnp.asarray((np.arange(s) // NSA_SEL_BLOCK)[:, None] == np.arange(nsp)[None, :], BF16)
    cols = NSA_REP * tq
    gate_rows = gs.shape[2]

    def qblk(rows):
        return pl.BlockSpec((1, 1, 1, rows, cols), lambda bi, gi, i: (bi, gi, i, 0, 0))

    def whole(n, w=dh):
        return pl.BlockSpec((1, 1, n, w), lambda bi, gi, i: (bi, gi, 0, 0))

    def full(shape):
        return pl.BlockSpec(shape, lambda bi, gi, i: (0,) * len(shape))

    return pl.pallas_call(
        functools.partial(_nsa_body, top_n=top_n),
        name="nsaattn",
        grid=(b, g, nq),
        in_specs=[qblk(dh), qblk(LANES), whole(nc), whole(dh, nc), whole(s, LANES), whole(dh, s),
                  whole(s, LANES), whole(dh, s),
                  pl.BlockSpec((1, 1, gate_rows, tq), lambda bi, gi, i: (bi, gi, 0, i)),
                  full((nsp, nc)), full((s, nsp))],
        out_specs=pl.BlockSpec((tq, NSA_REP * dh), lambda bi, gi, i: (bi * nq + i, gi)),
        out_shape=jax.ShapeDtypeStruct((b * s, NSA_WIDTH), BF16),
        scratch_shapes=[pltpu.VMEM((1, cols), F32), pltpu.VMEM((1, cols), F32), pltpu.VMEM((dh, cols), F32),
                        pltpu.VMEM((NSA_KT, cols), F32), pltpu.VMEM((NSA_KT, cols), F32)],
        compiler_params=_cparams(("parallel", "parallel", "arbitrary"), 56),
    )(qn, qr, kc, vc, ks, vs, kw, vw, gs, c2st, et)


HG_TILE = 256


def _hgrn_body(q_ref, f_ref, i_ref, g_ref, lbl_ref, ng_ref, tri_ref, o_ref,
               st_ref, gc_ref, k_ref, oacc_ref, *, layer):
    sub = HG_SUB
    dk = HG_HEAD_DIM
    th = q_ref.shape[0]

    @pl.when(pl.program_id(1) == 0)
    def _():
        st_ref[...] = jnp.zeros_like(st_ref)

    lbl = lbl_ref[...]
    el = jnp.exp(lbl - jnp.max(lbl, axis=0, keepdims=True))
    soft = el / jnp.sum(el, axis=0, keepdims=True)
    lb = jnp.zeros_like(soft[0:1])
    for d in range(1, layer + 1):
        lb = lb + soft[d:d + 1]

    f = lb + (1.0 - lb) * _sigmoid(f_ref[...])
    lf = jnp.log(f) * math.log2(math.e)
    k_ref[...] = 1.0 - f
    blk = tri_ref.shape[0]
    for c in range(th // blk):
        gc_ref[c * blk:(c + 1) * blk, :] = sum(
            _dot(tri_ref[...], piece) for piece in _split3(lf[c * blk:(c + 1) * blk, :]))

    srow = lax.broadcasted_iota(jnp.int32, (sub, 1), 0)

    def step(c, carry):
        r0 = pl.multiple_of(c * sub, sub)
        gall = gc_ref[pl.ds(r0, sub), :]
        qall = q_ref[pl.ds(r0, sub), :].astype(F32)
        kall = k_ref[pl.ds(r0, sub), :]
        vall = i_ref[pl.ds(r0, sub), :]
        outs = []
        for h in range(HG_HEADS):
            cs = slice(h * dk, (h + 1) * dk)
            g, q, kk, v = gall[:, cs], qall[:, cs], kall[:, cs], vall[:, cs]
            st = st_ref[h]
            o = _dot_nt((q * jnp.exp2(g)).astype(BF16), st.astype(BF16))
            parts = []
            for t0 in range(0, sub, SUBLANES):
                n = t0 + SUBLANES
                gs_, ks_, vs_, rows = g[0:n], kk[0:n], v[0:n], srow[0:n]
                acc = jnp.zeros((SUBLANES, dk), F32)
                for t in range(t0, n):
                    d = jnp.where(rows <= t, g[t:t + 1, :] - gs_, NEG)
                    w = (q[t:t + 1, :] * ks_) * jnp.exp2(d)
                    r = jnp.sum(w, axis=-1, keepdims=True)
                    ot = jnp.sum(r * vs_, axis=0, keepdims=True)
                    acc = acc + jnp.where(srow[0:SUBLANES] == t - t0, ot, 0.0)
                parts.append(acc)
            o = o + jnp.concatenate(parts, axis=0)
            g_last = g[sub - 1:sub, :]
            kt = kk * jnp.exp2(g_last - g)
            st_ref[h] = st * jnp.exp2(g_last) + _dot_tn(v.astype(BF16), kt.astype(BF16))
            outs.append(o)
        oacc_ref[pl.ds(r0, sub), :] = jnp.concatenate(outs, axis=1)
        return carry

    lax.fori_loop(0, th // sub, step, 0, unroll=4)

    o = oacc_ref[...]
    parts = []
    for h in range(HG_HEADS):
        oh = o[:, h * dk:(h + 1) * dk]
        ms = jnp.mean(oh * oh, axis=-1, keepdims=True)
        parts.append(oh * lax.rsqrt(ms + EPS) * ng_ref[...])
    o_ref[...] = (jnp.concatenate(parts, axis=1) * _silu(g_ref[...].astype(F32))).astype(o_ref.dtype)


def _hgrn(p, pf, lb_logits, norm_g, layer, b, s):
    th = HG_TILE
    nt = s // th
    blk = 64
    tri = jnp.asarray(np.kron(np.eye(blk // HG_SUB), np.tril(np.ones((HG_SUB, HG_SUB)))), BF16)
    depth = lb_logits.shape[0]

    def col(cb):
        return pl.BlockSpec((th, HG_WIDTH), lambda bi, i: (bi * nt + i, cb))

    def full(shape):
        return pl.BlockSpec(shape, lambda bi, i: (0,) * len(shape))

    return pl.pallas_call(
        functools.partial(_hgrn_body, layer=layer),
        name="hgrn",
        grid=(b, nt),
        in_specs=[col(C_HQ // HG_WIDTH), col(F_HF // HG_WIDTH), col(F_HI // HG_WIDTH), col(C_HG // HG_WIDTH),
                  full((depth, HG_WIDTH)), full((1, HG_HEAD_DIM)), full((blk, blk))],
        out_specs=pl.BlockSpec((th, HG_WIDTH), lambda bi, i: (bi * nt + i, 0)),
        out_shape=jax.ShapeDtypeStruct((b * s, HG_WIDTH), BF16),
        scratch_shapes=[pltpu.VMEM((HG_HEADS, HG_HEAD_DIM, HG_HEAD_DIM), F32),
                        pltpu.VMEM((th, HG_WIDTH), F32),
                        pltpu.VMEM((th, HG_WIDTH), F32),
                        pltpu.VMEM((th, HG_WIDTH), F32)],
        compiler_params=_cparams(("parallel", "arbitrary"), 40),
    )(p, pf, pf, p, lb_logits, norm_g[None, :], tri)


def _m2_body(z_ref, xs_ref, b_ref, c_ref, small_ref, cwx_ref, cwbc_ref, cbx_ref, cbbc_ref,
             dtb_ref, alog_ref, dskip_ref, ng_ref, tri_ref, trit_ref, e16_ref, o_ref,
             st_ref, px_ref, pbc_ref):
    ch = M2_CHUNK
    hp = M2_HEAD_DIM
    ns = M2_STATE
    halo = SUBLANES

    @pl.when(pl.program_id(1) == 0)
    def _():
        st_ref[...] = jnp.zeros_like(st_ref)
        px_ref[0:halo, :] = jnp.zeros((halo, M2_INNER), F32)
        pbc_ref[0:halo, :] = jnp.zeros((halo, 2 * M2_BC), F32)

    px_ref[halo:halo + ch, :] = xs_ref[...].astype(F32)
    pbc_ref[halo:halo + ch, 0:M2_BC] = b_ref[...].astype(F32)
    pbc_ref[halo:halo + ch, M2_BC:2 * M2_BC] = c_ref[...].astype(F32)

    def conv(p_ref, w_ref, bias_ref):
        full = p_ref[...]
        acc = bias_ref[...] + w_ref[M2_CONV - 1:M2_CONV, :] * full[halo:halo + ch, :]
        for back in range(1, M2_CONV):
            k = M2_CONV - 1 - back
            acc = acc + w_ref[k:k + 1, :] * pltpu.roll(full, back, 0)[halo:halo + ch, :]
        return _silu(acc)

    xs = conv(px_ref, cwx_ref, cbx_ref)
    bc = conv(pbc_ref, cwbc_ref, cbbc_ref)
    px_ref[0:halo, :] = px_ref[ch:ch + halo, :]
    pbc_ref[0:halo, :] = pbc_ref[ch:ch + halo, :]

    dtr = small_ref[...] + dtb_ref[...]
    dt = jnp.maximum(dtr, 0.0) + jnp.log(1.0 + jnp.exp(-jnp.abs(dtr)))
    a = dt * (-jnp.exp(alog_ref[...]))
    a3 = _split3(a)
    a_cs = sum(_dot(tri_ref[...], piece) for piece in a3)
    a_cs_t = sum(_dot_tn(piece, trit_ref[...]) for piece in a3)
    e16 = e16_ref[...]

    dt_x = _dot_split3(dt, e16)
    acs_x = _dot_split3(a_cs, e16)
    alast_x = acs_x[ch - 1:ch, :]
    xdt = xs * dt_x
    xdec = (xdt * jnp.exp(alast_x - acs_x)).astype(BF16)
    xdt_b = xdt.astype(BF16)
    dec_out_x = jnp.exp(acs_x)
    dec_chunk_x = jnp.exp(alast_x)

    li = lax.broadcasted_iota(jnp.int32, (ch, ch), 0)
    si = lax.broadcasted_iota(jnp.int32, (ch, ch), 1)
    tril = li >= si
    ys = []
    hpg = M2_HEADS // M2_GROUPS
    gw = hpg * hp
    for g in range(M2_GROUPS):
        gs_ = slice(g * gw, (g + 1) * gw)
        bm = bc[:, g * ns:(g + 1) * ns].astype(BF16)
        cm = bc[:, M2_BC + g * ns:M2_BC + (g + 1) * ns].astype(BF16)
        cb = _dot_nt(cm, bm)
        st = st_ref[g]
        ys.append(_dot(cm, st.astype(BF16)) * dec_out_x[:, gs_])
        st_ref[g] = st * dec_chunk_x[:, gs_] + _dot_tn(bm, xdec[:, gs_])
        for hh in range(hpg):
            h = g * hpg + hh
            hl = SMALL_DT + h
            seg = jnp.where(tril, a_cs[:, hl:hl + 1] - a_cs_t[hl:hl + 1, :], NEG)
            ys.append(_dot((cb * jnp.exp(seg)).astype(BF16), xdt_b[:, h * hp:(h + 1) * hp]))
    nd = 1 + hpg
    y_off = jnp.concatenate([ys[g * nd] for g in range(M2_GROUPS)], axis=1)
    y_diag = jnp.concatenate([ys[g * nd + 1 + hh] for g in range(M2_GROUPS) for hh in range(hpg)], axis=1)
    y = y_diag + y_off + dskip_ref[...] * xs
    y = y * _silu(z_ref[...].astype(F32))
    gw = M2_INNER // M2_GROUPS
    parts = []
    for g in range(M2_GROUPS):
        yg = y[:, g * gw:(g + 1) * gw]
        ms = jnp.mean(yg * yg, axis=-1, keepdims=True)
        parts.append(yg * lax.rsqrt(ms + EPS))
    o_ref[...] = (jnp.concatenate(parts, axis=1) * ng_ref[...]).astype(o_ref.dtype)


def _mamba2(p, pf, conv_w, conv_b, dt_bias, a_log, d_skip, norm_g, b, s):
    ch = M2_CHUNK
    nt = s // ch
    tri = jnp.asarray(np.tril(np.ones((ch, ch))), BF16)
    spread = np.zeros((LANES, M2_INNER))
    spread[SMALL_DT:SMALL_DT + M2_HEADS] = np.kron(np.eye(M2_HEADS), np.ones((1, M2_HEAD_DIM)))
    e16 = jnp.asarray(spread, BF16)

    def lanes(v):
        return jnp.pad(v, (SMALL_DT, LANES - SMALL_DT - M2_HEADS))[None, :]

    def col(width, cb):
        return pl.BlockSpec((ch, width), lambda bi, i: (bi * nt + i, cb))

    def full(shape):
        return pl.BlockSpec(shape, lambda bi, i: (0,) * len(shape))

    return pl.pallas_call(
        _m2_body,
        name="mamba",
        grid=(b, nt),
        in_specs=[col(M2_INNER, C_Z // M2_INNER), col(M2_INNER, C_XS // M2_INNER),
                  col(M2_BC, C_B // M2_BC), col(M2_BC, C_C // M2_BC), col(LANES, F_SMALL // LANES),
                  full((M2_CONV, M2_INNER)), full((M2_CONV, 2 * M2_BC)),
                  full((1, M2_INNER)), full((1, 2 * M2_BC)),
                  full((1, LANES)), full((1, LANES)), full((1, M2_INNER)), full((1, M2_INNER)),
                  full((ch, ch)), full((ch, ch)), full((LANES, M2_INNER))],
        out_specs=pl.BlockSpec((ch, M2_INNER), lambda bi, i: (bi * nt + i, 0)),
        out_shape=jax.ShapeDtypeStruct((b * s, M2_INNER), BF16),
        scratch_shapes=[pltpu.VMEM((M2_GROUPS, M2_STATE, M2_INNER // M2_GROUPS), F32),
                        pltpu.VMEM((ch + SUBLANES, M2_INNER), F32),
                        pltpu.VMEM((ch + SUBLANES, 2 * M2_BC), F32)],
        compiler_params=_cparams(("parallel", "arbitrary"), 40),
    )(p, p, p, p, pf,
      conv_w[:, :M2_INNER], conv_w[:, M2_INNER:], conv_b[None, :M2_INNER], conv_b[None, M2_INNER:],
      lanes(dt_bias), lanes(a_log), jnp.repeat(d_skip, M2_HEAD_DIM)[None, :], norm_g[None, :], tri, tri.T, e16)


def _merge_body(x_ref, ya_ref, yb_ref, yc_ref, ga_ref, gb_ref, gc_ref, wa_ref, wb_ref, wc_ref, wo_ref, o_ref):
    merged = (_sigmoid(ga_ref[...].astype(F32)) * _dot(ya_ref[...], wa_ref[...])
              + _sigmoid(gb_ref[...].astype(F32)) * _dot(yb_ref[...], wb_ref[...])
              + _sigmoid(gc_ref[...].astype(F32)) * _dot(yc_ref[...], wc_ref[...]))
    o_ref[...] = x_ref[...] + _dot(merged.astype(BF16), wo_ref[...])


def _merge(x2, ya, yb, yc, p, wa, wb, wc, wo):
    n = x2.shape[0]
    tm = 256

    def rows(width, cb=0):
        return pl.BlockSpec((tm, width), lambda i: (i, cb))

    def full(shape):
        return pl.BlockSpec(shape, lambda i: (0,) * len(shape))

    return pl.pallas_call(
        _merge_body,
        name="merge",
        grid=(n // tm,),
        in_specs=[rows(D_MODEL), rows(NSA_WIDTH), rows(HG_WIDTH), rows(M2_INNER),
                  rows(D_MODEL, C_GA // D_MODEL), rows(D_MODEL, C_GB // D_MODEL), rows(D_MODEL, C_GC // D_MODEL),
                  full((NSA_WIDTH, D_MODEL)), full((HG_WIDTH, D_MODEL)), full((M2_INNER, D_MODEL)),
                  full((D_MODEL, D_MODEL))],
        out_specs=rows(D_MODEL),
        out_shape=jax.ShapeDtypeStruct((n, D_MODEL), F32),
        compiler_params=_cparams(("parallel",), 48),
    )(x2, ya, yb, yc, p, p, p, wa.astype(BF16), wb.astype(BF16), wc.astype(BF16), wo.astype(BF16))


FFN_FT = 1408


def _ffn_body(x_ref, xh_ref, g_ref, wg_ref, wu_ref, cwg_ref, cwu_ref, cbg_ref, cbu_ref, wd_ref, o_ref,
              h_ref, ug_ref, uu_ref, acc_ref, *, tiles_per_seq):
    halo = SUBLANES
    tm = x_ref.shape[0]
    j = pl.program_id(1)

    def norm(x):
        ms = jnp.mean(x * x, axis=-1, keepdims=True)
        return (x * lax.rsqrt(ms + EPS) * g_ref[...]).astype(BF16)

    @pl.when(j == 0)
    def _():
        first = (pl.program_id(0) % tiles_per_seq) == 0
        h_ref[0:halo, :] = jnp.where(first, jnp.zeros((halo, D_MODEL), BF16), norm(xh_ref[...]))
        h_ref[halo:halo + tm, :] = norm(x_ref[...])
        acc_ref[...] = jnp.zeros_like(acc_ref)

    h = h_ref[...]
    ug_ref[...] = _dot(h, wg_ref[...])
    uu_ref[...] = _dot(h, wu_ref[...])

    def conv(u_ref, w_ref, bias_ref):
        acc = bias_ref[...]
        for k in range(FFN_CONV):
            off = halo - (FFN_CONV - 1) + k
            acc = acc + w_ref[k:k + 1, :] * u_ref[off:off + tm, :]
        return acc

    act = _silu(conv(ug_ref, cwg_ref, cbg_ref)) * conv(uu_ref, cwu_ref, cbu_ref)
    acc_ref[...] += _dot(act.astype(BF16), wd_ref[...])

    @pl.when(j == pl.num_programs(1) - 1)
    def _():
        o_ref[...] = x_ref[...] + acc_ref[...]


def _conv_ffn(x2, g, w_up, conv_w, conv_b, w_down, s):
    n = x2.shape[0]
    tm = 512
    ft = FFN_FT
    nf = FFN_DIM // ft
    halo = SUBLANES
    hb = tm // halo
    w_up = w_up.astype(BF16)
    return pl.pallas_call(
        functools.partial(_ffn_body, tiles_per_seq=s // tm),
        name="convffn",
        grid=(n // tm, nf),
        in_specs=[pl.BlockSpec((tm, D_MODEL), lambda i, j: (i, 0)),
                  pl.BlockSpec((halo, D_MODEL), lambda i, j: (jnp.maximum(i * hb - 1, 0), 0)),
                  pl.BlockSpec((1, D_MODEL), lambda i, j: (0, 0)),
                  pl.BlockSpec((D_MODEL, ft), lambda i, j: (0, j)),
                  pl.BlockSpec((D_MODEL, ft), lambda i, j: (0, nf + j)),
                  pl.BlockSpec((FFN_CONV, ft), lambda i, j: (0, j)),
                  pl.BlockSpec((FFN_CONV, ft), lambda i, j: (0, nf + j)),
                  pl.BlockSpec((1, ft), lambda i, j: (0, j)),
                  pl.BlockSpec((1, ft), lambda i, j: (0, nf + j)),
                  pl.BlockSpec((ft, D_MODEL), lambda i, j: (j, 0))],
        out_specs=pl.BlockSpec((tm, D_MODEL), lambda i, j: (i, 0)),
        out_shape=jax.ShapeDtypeStruct((n, D_MODEL), F32),
        scratch_shapes=[pltpu.VMEM((tm + halo, D_MODEL), BF16),
                        pltpu.VMEM((tm + halo, ft), F32),
                        pltpu.VMEM((tm + halo, ft), F32),
                        pltpu.VMEM((tm, D_MODEL), F32)],
        compiler_params=_cparams(("parallel", "arbitrary"), 56),
    )(x2, x2, g, w_up, w_up, conv_w, conv_w, conv_b[None, :], conv_b[None, :], w_down.astype(BF16))


def _rope_angles(positions):
    half = ROPE_DIM // 2
    inv_freq = ROPE_THETA ** (-jnp.arange(0, ROPE_DIM, 2, dtype=F32) / ROPE_DIM)
    lane = np.arange(LANES) % NSA_HEAD_DIM
    freq = jnp.where(jnp.asarray(lane < ROPE_DIM), inv_freq[jnp.asarray(lane % half)], 0.0)
    return positions.astype(F32).reshape(-1, 1) * freq[None, :]


def kernel(x, positions, attn_norm_g, ffn_norm_g, w_in, nsa_q_norm_g, nsa_k_norm_g, nsa_cmp_pos_k, nsa_cmp_pos_v, nsa_cmp_k_w1, nsa_cmp_k_w2, nsa_cmp_v_w1, nsa_cmp_v_w2, hgrn_lb_logits, hgrn_norm_g, m2_conv_w, m2_conv_b, m2_dt_bias, m2_a_log, m2_d_skip, m2_norm_g, w_branch_nsa, w_branch_hgrn, w_branch_m2, w_out, ffn_w_up, ffn_conv_w, ffn_conv_b, ffn_w_down):
    b, s, _ = x.shape
    depth = w_in.shape[0]
    x2 = x.reshape(b * s, D_MODEL)
    rope = _ropetab(_rope_angles(positions))
    for l in range(depth):
        p, pf = _inproj(x2, attn_norm_g[l][None, :], _pack_w_in(w_in[l]))
        qn, qr, ks, vs, kw, vw, gs = _nsaprep(p, pf, rope, nsa_q_norm_g[l], nsa_k_norm_g[l], b, s)
        kc, vc = _compress(pf, nsa_cmp_pos_k[l], nsa_cmp_pos_v[l], nsa_cmp_k_w1[l], nsa_cmp_k_w2[l],
                           nsa_cmp_v_w1[l], nsa_cmp_v_w2[l], nsa_k_norm_g[l, 0], b, s)
        ya = _nsa_attention(qn, qr, kc, vc, ks, vs, kw, vw, gs, b, s)
        yb = _hgrn(p, pf, hgrn_lb_logits, hgrn_norm_g[l], l, b, s)
        yc = _mamba2(p, pf, m2_conv_w[l], m2_conv_b[l], m2_dt_bias[l], m2_a_log[l], m2_d_skip[l], m2_norm_g[l], b, s)
        x2 = _merge(x2, ya, yb, yc, p, w_branch_nsa[l], w_branch_hgrn[l], w_branch_m2[l], w_out[l])
        x2 = _conv_ffn(x2, ffn_norm_g[l][None, :], ffn_w_up[l], ffn_conv_w[l], ffn_conv_b[l], ffn_w_down[l], s)
    return x2.reshape(b, s, D_MODEL)
```

```python
import functools
import math

import numpy as np
import jax
import jax.numpy as jnp
from jax import lax
from jax.experimental import pallas as pl
from jax.experimental.pallas import tpu as pltpu

F32 = jnp.float32
BF16 = jnp.bfloat16

D_MODEL = 1024
NSA_HEADS = 8
NSA_KV_HEADS = 2
NSA_REP = NSA_HEADS // NSA_KV_HEADS
NSA_HEAD_DIM = 64
NSA_CMP_BLOCK = 32
NSA_CMP_STRIDE = 16
NSA_SEL_BLOCK = 64
NSA_TOP_N = 16
NSA_WINDOW = 512
NSA_CMP_HIDDEN = 256
NSA_Q_BLOCK = 256
NSA_BIG = 1e9
ROPE_THETA = 500000.0
ROPE_DIM = NSA_HEAD_DIM // 4
NSA_WIDTH = NSA_HEADS * NSA_HEAD_DIM
NSA_KV_WIDTH = NSA_KV_HEADS * NSA_HEAD_DIM
HG_HEADS = 4
HG_HEAD_DIM = 128
HG_WIDTH = HG_HEADS * HG_HEAD_DIM
HG_SUB = 16
M2_HEADS = 16
M2_HEAD_DIM = 64
M2_INNER = M2_HEADS * M2_HEAD_DIM
M2_GROUPS = 2
M2_STATE = 128
M2_CONV = 4
M2_CHUNK = 128
M2_BC = M2_GROUPS * M2_STATE
FFN_DIM = 2816
FFN_CONV = 3
EPS = 1e-6

NEG = -1e30
NSA_Q_SCALE = NSA_HEAD_DIM ** -0.5 * math.log2(math.e)
LANES = 128
SUBLANES = 8

C_Z = 0
C_XS = 1024
C_GA = 2048
C_GB = 3072
C_GC = 4096
C_Q = 5120
C_HQ = 5632
C_HF = 6144
C_HI = 6656
C_HG = 7168
C_B = 7680
C_C = 7936
C_KV = 8192
C_SMALL = 8960
P_DIM = 9216
SMALL_DT = 3 * NSA_HEADS
P_TILE = 1024
F32_TILES = (C_HF // P_TILE, C_KV // P_TILE)
F_HF = 0
F_HI = C_HI - C_HF
F_KV = P_TILE
F_SMALL = P_TILE + C_SMALL - C_KV
F_DIM = 2 * P_TILE

_SRC = np.cumsum([0, NSA_WIDTH, 6 * NSA_KV_WIDTH, 3 * NSA_HEADS, HG_WIDTH, HG_WIDTH, HG_WIDTH, HG_WIDTH,
                  M2_INNER, M2_INNER + 2 * M2_BC, M2_HEADS, 3 * D_MODEL]).tolist()


def _cparams(sem, vmem_mib):
    return pltpu.CompilerParams(dimension_semantics=sem, vmem_limit_bytes=vmem_mib * 1024 * 1024)


def _sigmoid(x):
    return 1.0 / (1.0 + jnp.exp2(x * -math.log2(math.e)))


def _silu(x):
    return x * _sigmoid(x)


def _dot(a, b):
    return jnp.dot(a, b, preferred_element_type=F32)


def _dot_nt(a, b):
    return lax.dot_general(a, b, (((1,), (1,)), ((), ())), preferred_element_type=F32)


def _dot_tn(a, b):
    return lax.dot_general(a, b, (((0,), (0,)), ((), ())), preferred_element_type=F32)


def _split3(a):
    hi = a.astype(BF16)
    r1 = a - hi.astype(F32)
    mid = r1.astype(BF16)
    lo = (r1 - mid.astype(F32)).astype(BF16)
    return hi, mid, lo


def _dot_split3(a, sel):
    hi, mid, lo = _split3(a)
    return _dot(hi, sel) + _dot(mid, sel) + _dot(lo, sel)


def _pack_w_in(w):
    w = w.astype(BF16)
    o = _SRC
    xbc = o[8]
    pieces = [
        w[:, o[7]:o[8]],
        w[:, xbc:xbc + M2_INNER],
        w[:, o[10]:o[10] + D_MODEL],
        w[:, o[10] + D_MODEL:o[10] + 2 * D_MODEL],
        w[:, o[10] + 2 * D_MODEL:o[11]],
        w[:, o[0]:o[1]],
        w[:, o[3]:o[4]], w[:, o[4]:o[5]], w[:, o[5]:o[6]], w[:, o[6]:o[7]],
        w[:, xbc + M2_INNER:xbc + M2_INNER + M2_BC],
        w[:, xbc + M2_INNER + M2_BC:o[9]],
        w[:, o[1]:o[2]],
        w[:, o[2]:o[3]],
        w[:, o[9]:o[10]],
    ]
    pieces.append(jnp.zeros((w.shape[0], P_DIM - sum(pc.shape[1] for pc in pieces)), BF16))
    return jnp.concatenate(pieces, axis=1)


def _inproj_body(x_ref, g_ref, w_ref, o_ref, of_ref, h_ref):
    j = pl.program_id(1)

    @pl.when(j == 0)
    def _():
        x = x_ref[...]
        ms = jnp.mean(x * x, axis=-1, keepdims=True)
        h_ref[...] = (x * lax.rsqrt(ms + EPS) * g_ref[...]).astype(BF16)

    acc = _dot(h_ref[...], w_ref[...])
    o_ref[...] = acc.astype(BF16)

    @pl.when((j == F32_TILES[0]) | (j == F32_TILES[1]))
    def _():
        of_ref[...] = acc


def _inproj(x2, g, w):
    n = x2.shape[0]
    tm = 1024 if n % 1024 == 0 else 512
    tn = P_TILE
    return pl.pallas_call(
        _inproj_body,
        name="inproj",
        grid=(n // tm, P_DIM // tn),
        in_specs=[pl.BlockSpec((tm, D_MODEL), lambda i, j: (i, 0)),
                  pl.BlockSpec((1, D_MODEL), lambda i, j: (0, 0)),
                  pl.BlockSpec((D_MODEL, tn), lambda i, j: (0, j))],
        out_specs=(pl.BlockSpec((tm, tn), lambda i, j: (i, j)),
                   pl.BlockSpec((tm, tn), lambda i, j: (i, jnp.where(j <= F32_TILES[0], 0, 1)))),
        out_shape=(jax.ShapeDtypeStruct((n, P_DIM), BF16), jax.ShapeDtypeStruct((n, F_DIM), F32)),
        scratch_shapes=[pltpu.VMEM((tm, D_MODEL), BF16)],
        compiler_params=_cparams(("parallel", "arbitrary"), 48),
    )(x2, g, w)


def _head_sumsq(x, bd_ref):
    return _dot_split3(x * x, bd_ref[...])


def _rope(y, cos_t, sa_t, sb_t):
    w = y.shape[-1]
    return y * cos_t + pltpu.roll(y, w - ROPE_DIM // 2, 1) * sa_t + pltpu.roll(y, ROPE_DIM // 2, 1) * sb_t


def _ropetab_body(ang_ref, cos_o, sa_o, sb_o):
    half = ROPE_DIM // 2
    ang = ang_ref[...]
    lane = lax.broadcasted_iota(jnp.int32, ang.shape, 1) % NSA_HEAD_DIM
    sin_t = jnp.sin(ang)
    cos_o[...] = jnp.cos(ang)
    sa_o[...] = jnp.where(lane < half, -sin_t, 0.0)
    sb_o[...] = jnp.where((lane >= half) & (lane < ROPE_DIM), sin_t, 0.0)


def _ropetab(ang):
    n = ang.shape[0]
    t = 1024 if n % 1024 == 0 else 512
    blk = pl.BlockSpec((t, LANES), lambda i: (i, 0))
    sds = jax.ShapeDtypeStruct((n, LANES), F32)
    return pl.pallas_call(
        _ropetab_body,
        name="ropetab",
        grid=(n // t,),
        in_specs=[blk],
        out_specs=(blk, blk, blk),
        out_shape=(sds, sds, sds),
        compiler_params=_cparams(("parallel",), 32),
    )(ang)


def _nsaprep_body(q_ref, ksel_ref, vsel_ref, kwin_ref, vwin_ref, small_ref, cos_ref, sa_ref, sb_ref,
                  bdq_ref, bdk_ref, qg_ref, kg_ref,
                  qn_o, qr_o, ks_o, vs_o, kw_o, vw_o, gs_o):
    dh = NSA_HEAD_DIM
    cos_t = cos_ref[...]
    sa_t = sa_ref[...]
    sb_t = sb_ref[...]
    rep = NSA_WIDTH // LANES
    cos_q = jnp.concatenate([cos_t] * rep, axis=1)
    sa_q = jnp.concatenate([sa_t] * rep, axis=1)
    sb_q = jnp.concatenate([sb_t] * rep, axis=1)

    scale = NSA_Q_SCALE
    q = q_ref[...].astype(F32)
    qn = q * lax.rsqrt(_head_sumsq(q, bdq_ref) * (1.0 / dh) + EPS) * qg_ref[...]
    qr = _rope(qn, cos_q, sa_q, sb_q)
    tq = NSA_Q_BLOCK
    nrep = NSA_REP
    qn_t = (qn * scale).T
    qr_t = (qr * scale).T
    zrows = jnp.zeros((LANES - dh, nrep * tq), BF16)
    for g in range(NSA_KV_HEADS):
        for i in range(q.shape[0] // tq):
            def tile(a):
                return jnp.concatenate(
                    [a[(g * nrep + r) * dh:(g * nrep + r + 1) * dh, i * tq:(i + 1) * tq] for r in range(nrep)],
                    axis=1).astype(BF16)
            qn_o[0, g, i] = tile(qn_t)
            qr_o[0, g, i, 0:dh, :] = tile(qr_t)
            qr_o[0, g, i, dh:LANES, :] = zrows

    def knorm(k_ref, row):
        k = k_ref[...]
        kn = k * lax.rsqrt(_head_sumsq(k, bdk_ref) * (1.0 / dh) + EPS) * kg_ref[row:row + 1, :]
        return _rope(kn, cos_t, sa_t, sb_t).astype(BF16)

    ks = knorm(ksel_ref, 1)
    kw = knorm(kwin_ref, 2)
    vs_t = vsel_ref[...].T
    vw_t = vwin_ref[...].T
    zpad = jnp.zeros((q.shape[0], LANES - dh), BF16)
    sg = _sigmoid(small_ref[...])
    gate_rows = 4 * NSA_REP
    for g in range(NSA_KV_HEADS):
        ks_o[0, g] = jnp.concatenate([ks[:, g * dh:(g + 1) * dh], zpad], axis=1)
        kw_o[0, g] = jnp.concatenate([kw[:, g * dh:(g + 1) * dh], zpad], axis=1)
        vs_o[0, g] = vs_t[g * dh:(g + 1) * dh, :].astype(BF16)
        vw_o[0, g] = vw_t[g * dh:(g + 1) * dh, :].astype(BF16)
        sg_g = sg if g == 0 else pltpu.roll(sg, LANES - 3 * NSA_REP * g, 1)
        gs_o[0, g] = sg_g.T[0:gate_rows, :]


def _nsaprep(p, pf, rope, qg, kg, b, s):
    t = 512
    nt = s // t
    dh = NSA_HEAD_DIM
    bdq = jnp.asarray(np.kron(np.eye(NSA_HEADS), np.ones((dh, dh))), BF16)
    bdk = jnp.asarray(np.kron(np.eye(NSA_KV_HEADS), np.ones((dh, dh))), BF16)
    qg_t = jnp.tile(qg, NSA_HEADS)[None, :]
    kg_t = jnp.tile(kg, (1, NSA_KV_HEADS))
    kvb = F_KV // LANES

    def col(width, cb):
        return pl.BlockSpec((t, width), lambda bi, i: (bi * nt + i, cb))

    def full(shape):
        return pl.BlockSpec(shape, lambda bi, i: (0,) * len(shape))

    g = NSA_KV_HEADS
    tq = NSA_Q_BLOCK
    cols = NSA_REP * tq
    gate_rows = 4 * NSA_REP

    def qt(rows):
        return (jax.ShapeDtypeStruct((b, g, s // tq, rows, cols), BF16),
                pl.BlockSpec((1, g, t // tq, rows, cols), lambda bi, i: (bi, 0, i, 0, 0)))

    def token_rows(w):
        return (jax.ShapeDtypeStruct((b, g, s, w), BF16),
                pl.BlockSpec((1, g, t, w), lambda bi, i: (bi, 0, i, 0)))

    def token_lanes(rows, dtype):
        return (jax.ShapeDtypeStruct((b, g, rows, s), dtype),
                pl.BlockSpec((1, g, rows, t), lambda bi, i: (bi, 0, 0, i)))

    outs = [qt(dh), qt(LANES), token_rows(LANES), token_lanes(dh, BF16), token_rows(LANES), token_lanes(dh, BF16),
            token_lanes(gate_rows, F32)]
    out_shape = tuple(o[0] for o in outs)
    return pl.pallas_call(
        _nsaprep_body,
        name="nsaprep",
        grid=(b, nt),
        in_specs=[col(NSA_WIDTH, C_Q // NSA_WIDTH),
                  col(LANES, kvb + 2), col(LANES, kvb + 3), col(LANES, kvb + 4), col(LANES, kvb + 5),
                  col(LANES, F_SMALL // LANES),
                  col(LANES, 0), col(LANES, 0), col(LANES, 0),
                  full((NSA_WIDTH, NSA_WIDTH)), full((LANES, LANES)),
                  full((1, NSA_WIDTH)), full((3, LANES))],
        out_specs=tuple(o[1] for o in outs),
        out_shape=out_shape,
        compiler_params=_cparams(("parallel", "parallel"), 40),
    )(p, pf, pf, pf, pf, pf, rope[0], rope[1], rope[2], bdq, bdk, qg_t, kg_t)


def _compress_body(xk_ref, xv_ref, pk_ref, pv_ref, kw1_ref, kw2_ref, vw1_ref, vw2t_ref, kg_ref,
                   kc_o, vc_o):
    st = NSA_CMP_STRIDE
    nc = kc_o.shape[2]

    def hidden(x_ref, p_ref, w1_ref):
        top = bot = None
        for l in range(st):
            rows = x_ref[pl.ds(l, nc, stride=st), :].astype(BF16)
            t = _dot(rows, w1_ref[0, l * LANES:(l + 1) * LANES, :])
            u = _dot(rows, w1_ref[0, (st + l) * LANES:(st + l + 1) * LANES, :])
            top = t if top is None else top + t
            bot = u if bot is None else bot + u
        posb = _dot(p_ref[0], w1_ref[0])[0:1, :]
        hid = top + pltpu.roll(bot, nc - 1, 0) + posb
        return jax.nn.gelu(hid, approximate=True).astype(BF16)

    kc = _dot(hidden(xk_ref, pk_ref, kw1_ref), kw2_ref[...])
    ms = jnp.mean(kc * kc, axis=-1, keepdims=True)
    kc_o[0, 0] = (kc * lax.rsqrt(ms + EPS) * kg_ref[...]).astype(BF16)
    vc_o[0, 0] = _dot_nt(vw2t_ref[...], hidden(xv_ref, pv_ref, vw1_ref)).astype(BF16)


def _compress(p, pos_k, pos_v, kw1, kw2, vw1, vw2, kg0, b, s):
    dh = NSA_HEAD_DIM
    g = NSA_KV_HEADS
    nc = s // NSA_CMP_STRIDE
    kin = NSA_CMP_BLOCK * LANES
    kvb = F_KV // LANES

    def group_rows(a):
        a = a.reshape(NSA_CMP_BLOCK, dh, -1)
        return jnp.stack([jnp.pad(a, ((0, 0), (gi * dh, (g - 1 - gi) * dh), (0, 0))).reshape(kin, -1)
                          for gi in range(g)]).astype(BF16)

    def pos_rows(pe):
        return jnp.broadcast_to(jnp.swapaxes(group_rows(pe.reshape(-1, 1)), 1, 2), (g, SUBLANES, kin))

    def full(shape):
        return pl.BlockSpec(shape, lambda bi, gi: (0,) * len(shape))

    def per_group(shape):
        return pl.BlockSpec((1,) + shape, lambda bi, gi: (gi, 0, 0))

    oblk = pl.BlockSpec((1, 1, nc, dh), lambda bi, gi: (bi, gi, 0, 0))
    return pl.pallas_call(
        _compress_body,
        name="nsacompress",
        grid=(b, g),
        in_specs=[pl.BlockSpec((s, LANES), lambda bi, gi: (bi, kvb)),
                  pl.BlockSpec((s, LANES), lambda bi, gi: (bi, kvb + 1)),
                  per_group((SUBLANES, kin)), per_group((SUBLANES, kin)),
                  per_group((kin, NSA_CMP_HIDDEN)), full((NSA_CMP_HIDDEN, dh)),
                  per_group((kin, NSA_CMP_HIDDEN)), full((dh, NSA_CMP_HIDDEN)),
                  full((1, dh))],
        out_specs=(oblk, pl.BlockSpec((1, 1, dh, nc), lambda bi, gi: (bi, gi, 0, 0))),
        out_shape=(jax.ShapeDtypeStruct((b, g, nc, dh), BF16),
                   jax.ShapeDtypeStruct((b, g, dh, nc), BF16)),
        compiler_params=_cparams(("parallel", "arbitrary"), 48),
    )(p, p, pos_rows(pos_k), pos_rows(pos_v),
      group_rows(kw1), kw2.astype(BF16), group_rows(vw1), vw2.T.astype(BF16), kg0[None, :])


NSA_KT = 512


def _softmax_cols(s):
    m = jnp.max(s, axis=0, keepdims=True)
    e = jnp.exp2(s - m)
    l = jnp.sum(e, axis=0, keepdims=True)
    return e, jnp.where(m > 0.5 * NEG, 1.0 / l, 0.0)


def _nsa_body(qn_ref, qr_ref, kc_ref, vct_ref, ks_ref, vst_ref, kw_ref, vwt_ref, gs_ref, c2st_ref, et_ref,
              o_ref, m_ref, l_ref, acc_ref, sa_ref, sb_ref, *, top_n):
    tq = NSA_Q_BLOCK
    rep = NSA_REP
    dh = NSA_HEAD_DIM
    cols = rep * tq
    s0 = pl.program_id(2) * tq
    qn = qn_ref[0, 0, 0]
    qr = qr_ref[0, 0, 0]
    t_row = s0 + lax.broadcasted_iota(jnp.int32, (1, tq), 1)

    def heads(bias):
        return jnp.concatenate([bias] * rep, axis=1)

    nc = kc_ref.shape[2]
    cj = lax.broadcasted_iota(jnp.int32, (nc, 1), 0)
    cbias = jnp.where((cj * NSA_CMP_STRIDE + (NSA_CMP_BLOCK - 1)) <= t_row, 0.0, NEG)
    e, inv = _softmax_cols(_dot(kc_ref[0, 0], qn) + heads(cbias))
    p_cmp = e * inv
    o_cmp = _dot(vct_ref[0, 0], p_cmp.astype(BF16))

    psum = p_cmp[:, 0:tq]
    for r in range(1, rep):
        psum = psum + p_cmp[:, r * tq:(r + 1) * tq]
    c2st = c2st_ref[...]
    imp = sum(_dot(c2st, piece) for piece in _split3(psum))
    nsp = imp.shape[0]
    n_sel = ks_ref.shape[2] // NSA_SEL_BLOCK
    nb = lax.broadcasted_iota(jnp.int32, (nsp, 1), 0)
    cur = t_row // NSA_SEL_BLOCK
    forced = (nb == 0) | (nb == cur) | (nb == cur - 1)
    valid = nb * NSA_SEL_BLOCK <= t_row
    work = jnp.where(forced, NSA_BIG, jnp.where(valid, imp, -NSA_BIG))
    work = jnp.where(nb < n_sel, work, -jnp.inf)
    nbf = nb.astype(F32)
    for _ in range(top_n):
        mx = jnp.max(work, axis=0, keepdims=True)
        idx = jnp.min(jnp.where(work == mx, nbf, float(nsp)), axis=0, keepdims=True)
        work = jnp.where(nbf == idx, -jnp.inf, work)
    picked = (work == -jnp.inf) & (nb < n_sel)

    wlen = NSA_WINDOW + tq
    kstart = pl.multiple_of(jnp.maximum(s0 - NSA_WINDOW, 0), tq)
    kpos = kstart + lax.broadcasted_iota(jnp.int32, (wlen, 1), 0)
    back = lax.bitcast_convert_type(t_row - kpos, jnp.uint32)
    wbias = jnp.where(back < jnp.uint32(NSA_WINDOW), 0.0, NEG)
    ew, invw = _softmax_cols(_dot(kw_ref[0, 0, pl.ds(kstart, wlen), :], qr) + heads(wbias))
    o_win = _dot(vwt_ref[0, 0, :, pl.ds(kstart, wlen)], ew.astype(BF16)) * invw

    selbias = jnp.where(picked, 0.0, NEG).astype(BF16)
    qx = jnp.concatenate([heads(selbias), qr], axis=0)
    kt = NSA_KT
    m_ref[...] = jnp.full((1, cols), NEG, F32)
    l_ref[...] = jnp.zeros((1, cols), F32)
    acc_ref[...] = jnp.zeros((dh, cols), F32)

    def scores(j):
        k0 = pl.multiple_of(j * kt, kt)
        kx = jnp.concatenate([et_ref[pl.ds(k0, kt), :], ks_ref[0, 0, pl.ds(k0, kt), :]], axis=1)
        return _dot(kx, qx)

    def update(s, j):
        k0 = pl.multiple_of(j * kt, kt)
        m_old = m_ref[...]
        m_new = jnp.maximum(m_old, jnp.max(s, axis=0, keepdims=True))
        alpha = jnp.exp2(m_old - m_new)
        pe = jnp.exp2(s - m_new)
        l_ref[...] = alpha * l_ref[...] + jnp.sum(pe, axis=0, keepdims=True)
        acc_ref[...] = alpha * acc_ref[...] + _dot(vst_ref[0, 0, :, pl.ds(k0, kt)], pe.astype(BF16))
        m_ref[...] = m_new

    n_full = s0 // kt
    sa_ref[...] = scores(0)

    def tile_pair(jj, carry):
        j = 2 * jj
        sb_ref[...] = scores(j + 1)
        update(sa_ref[...], j)
        sa_ref[...] = scores(j + 2)
        update(sb_ref[...], j + 1)
        return carry

    lax.fori_loop(0, n_full // 2, tile_pair, 0)
    dcr = lax.broadcasted_iota(jnp.int32, (kt, tq), 0) - lax.broadcasted_iota(jnp.int32, (kt, tq), 1)
    dbias = heads(jnp.where(dcr <= s0 - n_full * kt, 0.0, NEG))

    @pl.when(n_full % 2 == 0)
    def _():
        update(sa_ref[...] + dbias, n_full)

    @pl.when(n_full % 2 == 1)
    def _():
        sb_ref[...] = scores(n_full)
        update(sa_ref[...], n_full - 1)
        update(sb_ref[...] + dbias, n_full)

    o_sel = acc_ref[...] * (1.0 / l_ref[...])

    gates = gs_ref[0, 0]

    def gate(c):
        return jnp.concatenate([gates[3 * r + c:3 * r + c + 1, :] for r in range(rep)], axis=1)

    o_t = gate(0) * o_cmp + gate(1) * o_sel + gate(2) * o_win
    pairs = [jnp.concatenate([o_t[:, r * tq:(r + 1) * tq], o_t[:, (r + 1) * tq:(r + 2) * tq]], axis=0).T
             for r in range(0, rep, 2)]
    o_ref[...] = jnp.concatenate(pairs, axis=1).astype(o_ref.dtype)


def _nsa_attention(qn, qr, kc, vc, ks, vs, kw, vw, gs, b, s):
    tq = NSA_Q_BLOCK
    dh = NSA_HEAD_DIM
    g = NSA_KV_HEADS
    nq = s // tq
    nc = s // NSA_CMP_STRIDE
    n_sel = s // NSA_SEL_BLOCK
    nsp = -(-n_sel // LANES) * LANES
    top_n = min(NSA_TOP_N, n_sel)
    c_start = np.arange(nc) * NSA_CMP_STRIDE
    s_start = np.arange(nsp) * NSA_SEL_BLOCK
    overlap = np.clip(np.minimum(c_start[:, None] + NSA_CMP_BLOCK, s_start[None, :] + NSA_SEL_BLOCK)
                      - np.maximum(c_start[:, None], s_start[None, :]), 0, None)
    c2st = jnp.asarray((overlap / NSA_CMP_BLOCK).T, BF16)
    et = jnp.asarray((np.arange(s) // NSA_SEL_BLOCK)[:, None] == np.arange(nsp)[None, :], BF16)
    cols = NSA_REP * tq
    gate_rows = gs.shape[2]

    def qblk(rows):
        return pl.BlockSpec((1, 1, 1, rows, cols), lambda bi, gi, i: (bi, gi, i, 0, 0))

    def whole(n, w=dh):
        return pl.BlockSpec((1, 1, n, w), lambda bi, gi, i: (bi, gi, 0, 0))

    def full(shape):
        return pl.BlockSpec(shape, lambda bi, gi, i: (0,) * len(shape))

    return pl.pallas_call(
        functools.partial(_nsa_body, top_n=top_n),
        name="nsaattn",
        grid=(b, g, nq),
        in_specs=[qblk(dh), qblk(LANES), whole(nc), whole(dh, nc), whole(s, LANES), whole(dh, s),
                  whole(s, LANES), whole(dh, s),
                  pl.BlockSpec((1, 1, gate_rows, tq), lambda bi, gi, i: (bi, gi, 0, i)),
                  full((nsp, nc)), full((s, nsp))],
        out_specs=pl.BlockSpec((tq, NSA_REP * dh), lambda bi, gi, i: (bi * nq + i, gi)),
        out_shape=jax.ShapeDtypeStruct((b * s, NSA_WIDTH), BF16),
        scratch_shapes=[pltpu.VMEM((1, cols), F32), pltpu.VMEM((1, cols), F32), pltpu.VMEM((dh, cols), F32),
                        pltpu.VMEM((NSA_KT, cols), F32), pltpu.VMEM((NSA_KT, cols), F32)],
        compiler_params=_cparams(("parallel", "parallel", "arbitrary"), 56),
    )(qn, qr, kc, vc, ks, vs, kw, vw, gs, c2st, et)


HG_TILE = 256


def _hgrn_body(q_ref, f_ref, i_ref, g_ref, lbl_ref, ng_ref, tri_ref, o_ref,
               st_ref, gc_ref, k_ref, oacc_ref, *, layer):
    sub = HG_SUB
    dk = HG_HEAD_DIM
    th = q_ref.shape[0]

    @pl.when(pl.program_id(1) == 0)
    def _():
        st_ref[...] = jnp.zeros_like(st_ref)

    lbl = lbl_ref[...]
    el = jnp.exp(lbl - jnp.max(lbl, axis=0, keepdims=True))
    soft = el / jnp.sum(el, axis=0, keepdims=True)
    lb = jnp.zeros_like(soft[0:1])
    for d in range(1, layer + 1):
        lb = lb + soft[d:d + 1]

    f = lb + (1.0 - lb) * _sigmoid(f_ref[...])
    lf = jnp.log(f) * math.log2(math.e)
    k_ref[...] = 1.0 - f
    blk = tri_ref.shape[0]
    for c in range(th // blk):
        gc_ref[c * blk:(c + 1) * blk, :] = sum(
            _dot(tri_ref[...], piece) for piece in _split3(lf[c * blk:(c + 1) * blk, :]))

    srow = lax.broadcasted_iota(jnp.int32, (sub, 1), 0)

    def step(c, carry):
        r0 = pl.multiple_of(c * sub, sub)
        gall = gc_ref[pl.ds(r0, sub), :]
        qall = q_ref[pl.ds(r0, sub), :].astype(F32)
        kall = k_ref[pl.ds(r0, sub), :]
        vall = i_ref[pl.ds(r0, sub), :]
        outs = []
        for h in range(HG_HEADS):
            cs = slice(h * dk, (h + 1) * dk)
            g, q, kk, v = gall[:, cs], qall[:, cs], kall[:, cs], vall[:, cs]
            st = st_ref[h]
            o = _dot_nt((q * jnp.exp2(g)).astype(BF16), st.astype(BF16))
            parts = []
            for t0 in range(0, sub, SUBLANES):
                n = t0 + SUBLANES
                gs_, ks_, vs_, rows = g[0:n], kk[0:n], v[0:n], srow[0:n]
                acc = jnp.zeros((SUBLANES, dk), F32)
                for t in range(t0, n):
                    d = jnp.where(rows <= t, g[t:t + 1, :] - gs_, NEG)
                    w = (q[t:t + 1, :] * ks_) * jnp.exp2(d)
                    r = jnp.sum(w, axis=-1, keepdims=True)
                    ot = jnp.sum(r * vs_, axis=0, keepdims=True)
                    acc = acc + jnp.where(srow[0:SUBLANES] == t - t0, ot, 0.0)
                parts.append(acc)
            o = o + jnp.concatenate(parts, axis=0)
            g_last = g[sub - 1:sub, :]
            kt = kk * jnp.exp2(g_last - g)
            st_ref[h] = st * jnp.exp2(g_last) + _dot_tn(v.astype(BF16), kt.astype(BF16))
            outs.append(o)
        oacc_ref[pl.ds(r0, sub), :] = jnp.concatenate(outs, axis=1)
        return carry

    lax.fori_loop(0, th // sub, step, 0, unroll=4)

    o = oacc_ref[...]
    parts = []
    for h in range(HG_HEADS):
        oh = o[:, h * dk:(h + 1) * dk]
        ms = jnp.mean(oh * oh, axis=-1, keepdims=True)
        parts.append(oh * lax.rsqrt(ms + EPS) * ng_ref[...])
    o_ref[...] = (jnp.concatenate(parts, axis=1) * _silu(g_ref[...].astype(F32))).astype(o_ref.dtype)


def _hgrn(p, pf, lb_logits, norm_g, layer, b, s):
    th = HG_TILE
    nt = s // th
    blk = 64
    tri = jnp.asarray(np.kron(np.eye(blk // HG_SUB), np.tril(np.ones((HG_SUB, HG_SUB)))), BF16)
    depth = lb_logits.shape[0]

    def col(cb):
        return pl.BlockSpec((th, HG_WIDTH), lambda bi, i: (bi * nt + i, cb))

    def full(shape):
        return pl.BlockSpec(shape, lambda bi, i: (0,) * len(shape))

    return pl.pallas_call(
        functools.partial(_hgrn_body, layer=layer),
        name="hgrn",
        grid=(b, nt),
        in_specs=[col(C_HQ // HG_WIDTH), col(F_HF // HG_WIDTH), col(F_HI // HG_WIDTH), col(C_HG // HG_WIDTH),
                  full((depth, HG_WIDTH)), full((1, HG_HEAD_DIM)), full((blk, blk))],
        out_specs=pl.BlockSpec((th, HG_WIDTH), lambda bi, i: (bi * nt + i, 0)),
        out_shape=jax.ShapeDtypeStruct((b * s, HG_WIDTH), BF16),
        scratch_shapes=[pltpu.VMEM((HG_HEADS, HG_HEAD_DIM, HG_HEAD_DIM), F32),
                        pltpu.VMEM((th, HG_WIDTH), F32),
                        pltpu.VMEM((th, HG_WIDTH), F32),
                        pltpu.VMEM((th, HG_WIDTH), F32)],
        compiler_params=_cparams(("parallel", "arbitrary"), 40),
    )(p, pf, pf, p, lb_logits, norm_g[None, :], tri)


def _m2_body(z_ref, xs_ref, b_ref, c_ref, small_ref, cwx_ref, cwbc_ref, cbx_ref, cbbc_ref,
             dtb_ref, alog_ref, dskip_ref, ng_ref, tri_ref, trit_ref, e16_ref, o_ref,
             st_ref, px_ref, pbc_ref):
    ch = M2_CHUNK
    hp = M2_HEAD_DIM
    ns = M2_STATE
    halo = SUBLANES

    @pl.when(pl.program_id(1) == 0)
    def _():
        st_ref[...] = jnp.zeros_like(st_ref)
        px_ref[0:halo, :] = jnp.zeros((halo, M2_INNER), F32)
        pbc_ref[0:halo, :] = jnp.zeros((halo, 2 * M2_BC), F32)

    px_ref[halo:halo + ch, :] = xs_ref[...].astype(F32)
    pbc_ref[halo:halo + ch, 0:M2_BC] = b_ref[...].astype(F32)
    pbc_ref[halo:halo + ch, M2_BC:2 * M2_BC] = c_ref[...].astype(F32)

    def conv(p_ref, w_ref, bias_ref):
        full = p_ref[...]
        acc = bias_ref[...] + w_ref[M2_CONV - 1:M2_CONV, :] * full[halo:halo + ch, :]
        for back in range(1, M2_CONV):
            k = M2_CONV - 1 - back
            acc = acc + w_ref[k:k + 1, :] * pltpu.roll(full, back, 0)[halo:halo + ch, :]
        return _silu(acc)

    xs = conv(px_ref, cwx_ref, cbx_ref)
    bc = conv(pbc_ref, cwbc_ref, cbbc_ref)
    px_ref[0:halo, :] = px_ref[ch:ch + halo, :]
    pbc_ref[0:halo, :] = pbc_ref[ch:ch + halo, :]

    dtr = small_ref[...] + dtb_ref[...]
    dt = jnp.maximum(dtr, 0.0) + jnp.log(1.0 + jnp.exp(-jnp.abs(dtr)))
    a = dt * (-math.log2(math.e) * jnp.exp(alog_ref[...]))
    a3 = _split3(a)
    a_cs = sum(_dot(tri_ref[...], piece) for piece in a3)
    a_cs_t = sum(_dot_tn(piece, trit_ref[...]) for piece in a3)
    e16 = e16_ref[...]

    dt_x = _dot_split3(dt, e16)
    acs_x = _dot_split3(a_cs, e16)
    alast_x = acs_x[ch - 1:ch, :]
    xdt = xs * dt_x
    xdec = (xdt * jnp.exp2(alast_x - acs_x)).astype(BF16)
    xdt_b = xdt.astype(BF16)
    dec_out_x = jnp.exp2(acs_x)
    dec_chunk_x = jnp.exp2(alast_x)

    li = lax.broadcasted_iota(jnp.int32, (ch, ch), 0)
    si = lax.broadcasted_iota(jnp.int32, (ch, ch), 1)
    tril = li >= si
    ys = []
    hpg = M2_HEADS // M2_GROUPS
    gw = hpg * hp
    for g in range(M2_GROUPS):
        gs_ = slice(g * gw, (g + 1) * gw)
        bm = bc[:, g * ns:(g + 1) * ns].astype(BF16)
        cm = bc[:, M2_BC + g * ns:M2_BC + (g + 1) * ns].astype(BF16)
        cb = _dot_nt(cm, bm)
        st = st_ref[g]
        ys.append(_dot(cm, st.astype(BF16)) * dec_out_x[:, gs_])
        st_ref[g] = st * dec_chunk_x[:, gs_] + _dot_tn(bm, xdec[:, gs_])
        for hh in range(hpg):
            h = g * hpg + hh
            hl = SMALL_DT + h
            seg = jnp.where(tril, a_cs[:, hl:hl + 1] - a_cs_t[hl:hl + 1, :], NEG)
            ys.append(_dot((cb * jnp.exp2(seg)).astype(BF16), xdt_b[:, h * hp:(h + 1) * hp]))
    nd = 1 + hpg
    y_off = jnp.concatenate([ys[g * nd] for g in range(M2_GROUPS)], axis=1)
    y_diag = jnp.concatenate([ys[g * nd + 1 + hh] for g in range(M2_GROUPS) for hh in range(hpg)], axis=1)
    y = y_diag + y_off + dskip_ref[...] * xs
    y = y * _silu(z_ref[...].astype(F32))
    gw = M2_INNER // M2_GROUPS
    parts = []
    for g in range(M2_GROUPS):
        yg = y[:, g * gw:(g + 1) * gw]
        ms = jnp.mean(yg * yg, axis=-1, keepdims=True)
        parts.append(yg * lax.rsqrt(ms + EPS))
    o_ref[...] = (jnp.concatenate(parts, axis=1) * ng_ref[...]).astype(o_ref.dtype)


def _mamba2(p, pf, conv_w, conv_b, dt_bias, a_log, d_skip, norm_g, b, s):
    ch = M2_CHUNK
    nt = s // ch
    tri = jnp.asarray(np.tril(np.ones((ch, ch))), BF16)
    spread = np.zeros((LANES, M2_INNER))
    spread[SMALL_DT:SMALL_DT + M2_HEADS] = np.kron(np.eye(M2_HEADS), np.ones((1, M2_HEAD_DIM)))
    e16 = jnp.asarray(spread, BF16)

    def lanes(v):
        return jnp.pad(v, (SMALL_DT, LANES - SMALL_DT - M2_HEADS))[None, :]

    def col(width, cb):
        return pl.BlockSpec((ch, width), lambda bi, i: (bi * nt + i, cb))

    def full(shape):
        return pl.BlockSpec(shape, lambda bi, i: (0,) * len(shape))

    return pl.pallas_call(
        _m2_body,
        name="mamba",
        grid=(b, nt),
        in_specs=[col(M2_INNER, C_Z // M2_INNER), col(M2_INNER, C_XS // M2_INNER),
                  col(M2_BC, C_B // M2_BC), col(M2_BC, C_C // M2_BC), col(LANES, F_SMALL // LANES),
                  full((M2_CONV, M2_INNER)), full((M2_CONV, 2 * M2_BC)),
                  full((1, M2_INNER)), full((1, 2 * M2_BC)),
                  full((1, LANES)), full((1, LANES)), full((1, M2_INNER)), full((1, M2_INNER)),
                  full((ch, ch)), full((ch, ch)), full((LANES, M2_INNER))],
        out_specs=pl.BlockSpec((ch, M2_INNER), lambda bi, i: (bi * nt + i, 0)),
        out_shape=jax.ShapeDtypeStruct((b * s, M2_INNER), BF16),
        scratch_shapes=[pltpu.VMEM((M2_GROUPS, M2_STATE, M2_INNER // M2_GROUPS), F32),
                        pltpu.VMEM((ch + SUBLANES, M2_INNER), F32),
                        pltpu.VMEM((ch + SUBLANES, 2 * M2_BC), F32)],
        compiler_params=_cparams(("parallel", "arbitrary"), 40),
    )(p, p, p, p, pf,
      conv_w[:, :M2_INNER], conv_w[:, M2_INNER:], conv_b[None, :M2_INNER], conv_b[None, M2_INNER:],
      lanes(dt_bias), lanes(a_log), jnp.repeat(d_skip, M2_HEAD_DIM)[None, :], norm_g[None, :], tri, tri.T, e16)


def _merge_body(x_ref, ya_ref, yb_ref, yc_ref, ga_ref, gb_ref, gc_ref, wa_ref, wb_ref, wc_ref, wo_ref, o_ref):
    merged = (_sigmoid(ga_ref[...].astype(F32)) * _dot(ya_ref[...], wa_ref[...])
              + _sigmoid(gb_ref[...].astype(F32)) * _dot(yb_ref[...], wb_ref[...])
              + _sigmoid(gc_ref[...].astype(F32)) * _dot(yc_ref[...], wc_ref[...]))
    o_ref[...] = x_ref[...] + _dot(merged.astype(BF16), wo_ref[...])


def _merge(x2, ya, yb, yc, p, wa, wb, wc, wo):
    n = x2.shape[0]
    tm = 256

    def rows(width, cb=0):
        return pl.BlockSpec((tm, width), lambda i: (i, cb))

    def full(shape):
        return pl.BlockSpec(shape, lambda i: (0,) * len(shape))

    return pl.pallas_call(
        _merge_body,
        name="merge",
        grid=(n // tm,),
        in_specs=[rows(D_MODEL), rows(NSA_WIDTH), rows(HG_WIDTH), rows(M2_INNER),
                  rows(D_MODEL, C_GA // D_MODEL), rows(D_MODEL, C_GB // D_MODEL), rows(D_MODEL, C_GC // D_MODEL),
                  full((NSA_WIDTH, D_MODEL)), full((HG_WIDTH, D_MODEL)), full((M2_INNER, D_MODEL)),
                  full((D_MODEL, D_MODEL))],
        out_specs=rows(D_MODEL),
        out_shape=jax.ShapeDtypeStruct((n, D_MODEL), F32),
        compiler_params=_cparams(("parallel",), 48),
    )(x2, ya, yb, yc, p, p, p, wa.astype(BF16), wb.astype(BF16), wc.astype(BF16), wo.astype(BF16))


FFN_FT = 1408


def _ffn_body(x_ref, xh_ref, g_ref, wg_ref, wu_ref, cwg_ref, cwu_ref, cbg_ref, cbu_ref, wd_ref, o_ref,
              h_ref, ug_ref, uu_ref, acc_ref, *, tiles_per_seq):
    halo = SUBLANES
    tm = x_ref.shape[0]
    j = pl.program_id(1)

    def norm(x):
        ms = jnp.mean(x * x, axis=-1, keepdims=True)
        return (x * lax.rsqrt(ms + EPS) * g_ref[...]).astype(BF16)

    @pl.when(j == 0)
    def _():
        first = (pl.program_id(0) % tiles_per_seq) == 0
        h_ref[0:halo, :] = jnp.where(first, jnp.zeros((halo, D_MODEL), BF16), norm(xh_ref[...]))
        h_ref[halo:halo + tm, :] = norm(x_ref[...])
        acc_ref[...] = jnp.zeros_like(acc_ref)

    h = h_ref[...]
    ug_ref[...] = _dot(h, wg_ref[...])
    uu_ref[...] = _dot(h, wu_ref[...])

    def conv(u_ref, w_ref, bias_ref):
        acc = bias_ref[...]
        for k in range(FFN_CONV):
            off = halo - (FFN_CONV - 1) + k
            acc = acc + w_ref[k:k + 1, :] * u_ref[off:off + tm, :]
        return acc

    act = _silu(conv(ug_ref, cwg_ref, cbg_ref)) * conv(uu_ref, cwu_ref, cbu_ref)
    acc_ref[...] += _dot(act.astype(BF16), wd_ref[...])

    @pl.when(j == pl.num_programs(1) - 1)
    def _():
        o_ref[...] = x_ref[...] + acc_ref[...]


def _conv_ffn(x2, g, w_up, conv_w, conv_b, w_down, s):
    n = x2.shape[0]
    tm = 512
    ft = FFN_FT
    nf = FFN_DIM // ft
    halo = SUBLANES
    hb = tm // halo
    w_up = w_up.astype(BF16)
    return pl.pallas_call(
        functools.partial(_ffn_body, tiles_per_seq=s // tm),
        name="convffn",
        grid=(n // tm, nf),
        in_specs=[pl.BlockSpec((tm, D_MODEL), lambda i, j: (i, 0)),
                  pl.BlockSpec((halo, D_MODEL), lambda i, j: (jnp.maximum(i * hb - 1, 0), 0)),
                  pl.BlockSpec((1, D_MODEL), lambda i, j: (0, 0)),
                  pl.BlockSpec((D_MODEL, ft), lambda i, j: (0, j)),
                  pl.BlockSpec((D_MODEL, ft), lambda i, j: (0, nf + j)),
                  pl.BlockSpec((FFN_CONV, ft), lambda i, j: (0, j)),
                  pl.BlockSpec((FFN_CONV, ft), lambda i, j: (0, nf + j)),
                  pl.BlockSpec((1, ft), lambda i, j: (0, j)),
                  pl.BlockSpec((1, ft), lambda i, j: (0, nf + j)),
                  pl.BlockSpec((ft, D_MODEL), lambda i, j: (j, 0))],
        out_specs=pl.BlockSpec((tm, D_MODEL), lambda i, j: (i, 0)),
        out_shape=jax.ShapeDtypeStruct((n, D_MODEL), F32),
        scratch_shapes=[pltpu.VMEM((tm + halo, D_MODEL), BF16),
                        pltpu.VMEM((tm + halo, ft), F32),
                        pltpu.VMEM((tm + halo, ft), F32),
                        pltpu.VMEM((tm, D_MODEL), F32)],
        compiler_params=_cparams(("parallel", "arbitrary"), 56),
    )(x2, x2, g, w_up, w_up, conv_w, conv_w, conv_b[None, :], conv_b[None, :], w_down.astype(BF16))


def _rope_angles(positions):
    half = ROPE_DIM // 2
    inv_freq = ROPE_THETA ** (-jnp.arange(0, ROPE_DIM, 2, dtype=F32) / ROPE_DIM)
    lane = np.arange(LANES) % NSA_HEAD_DIM
    freq = jnp.where(jnp.asarray(lane < ROPE_DIM), inv_freq[jnp.asarray(lane % half)], 0.0)
    return positions.astype(F32).reshape(-1, 1) * freq[None, :]


def kernel(x, positions, attn_norm_g, ffn_norm_g, w_in, nsa_q_norm_g, nsa_k_norm_g, nsa_cmp_pos_k, nsa_cmp_pos_v, nsa_cmp_k_w1, nsa_cmp_k_w2, nsa_cmp_v_w1, nsa_cmp_v_w2, hgrn_lb_logits, hgrn_norm_g, m2_conv_w, m2_conv_b, m2_dt_bias, m2_a_log, m2_d_skip, m2_norm_g, w_branch_nsa, w_branch_hgrn, w_branch_m2, w_out, ffn_w_up, ffn_conv_w, ffn_conv_b, ffn_w_down):
    b, s, _ = x.shape
    depth = w_in.shape[0]
    x2 = x.reshape(b * s, D_MODEL)
    rope = _ropetab(_rope_angles(positions))
    for l in range(depth):
        p, pf = _inproj(x2, attn_norm_g[l][None, :], _pack_w_in(w_in[l]))
        qn, qr, ks, vs, kw, vw, gs = _nsaprep(p, pf, rope, nsa_q_norm_g[l], nsa_k_norm_g[l], b, s)
        kc, vc = _compress(pf, nsa_cmp_pos_k[l], nsa_cmp_pos_v[l], nsa_cmp_k_w1[l], nsa_cmp_k_w2[l],
                           nsa_cmp_v_w1[l], nsa_cmp_v_w2[l], nsa_k_norm_g[l, 0], b, s)
        ya = _nsa_attention(qn, qr, kc, vc, ks, vs, kw, vw, gs, b, s)
        yb = _hgrn(p, pf, hgrn_lb_logits, hgrn_norm_g[l], l, b, s)
        yc = _mamba2(p, pf, m2_conv_w[l], m2_conv_b[l], m2_dt_bias[l], m2_a_log[l], m2_d_skip[l], m2_norm_g[l], b, s)
        x2 = _merge(x2, ya, yb, yc, p, w_branch_nsa[l], w_branch_hgrn[l], w_branch_m2[l], w_out[l])
        x2 = _conv_ffn(x2, ffn_norm_g[l][None, :], ffn_w_up[l], ffn_conv_w[l], ffn_conv_b[l], ffn_w_down[l], s)
    return x2.reshape(b, s, D_MODEL)
```
